```python
import math
import jax
import jax.numpy as jnp
from jax import lax
import numpy as np

D_MODEL = 2048
BATCH = 4
SEQ = 4096
DEPTH = 4

CHUNK = 64
LN_EPS = 1e-5
RET_HEADS = 8
RET_DK = D_MODEL // 16
RET_DV = D_MODEL // 8
RET_QK = RET_HEADS * RET_DK
RET_V = RET_HEADS * RET_DV
ROPE_BASE = 10000.0
SSD_DI = D_MODEL
SSD_P = 64
SSD_HEADS = SSD_DI // SSD_P
SSD_G = 4
SSD_N = 128
SSD_CONV = 4
SSD_XBC = SSD_DI + 2 * SSD_G * SSD_N
HG_DK = 128
HG_HEADS = D_MODEL // HG_DK
HG_DV = D_MODEL // HG_HEADS
HG_W = HG_HEADS * HG_DK
HG_V = HG_HEADS * HG_DV
FFN_DIM = ((8 * D_MODEL // 3 + 255) // 256) * 256
N_EXPERTS = 8
TOP_K = 2
MOE_BLOCK = 256

EV_SIZES = (RET_QK, RET_QK, RET_V, RET_V, SSD_DI, SSD_XBC, SSD_HEADS)
EV_IN = 2 * RET_QK + 2 * RET_V + SSD_DI + SSD_XBC + SSD_HEADS
OD_SIZES = (HG_W, HG_W, HG_V, HG_V)
OD_IN = 2 * HG_W + 2 * HG_V

kernel_name = "hybrid_retention_ssd_hgrn2_moe_encoder"

F32 = jnp.float32


def _split(t, sizes):
    return jnp.split(t, [int(v) for v in np.cumsum(sizes)[:-1]], axis=-1)


def layer_norm(x, g, b):
    x32 = x.astype(F32)
    mu = jnp.mean(x32, axis=-1, keepdims=True)
    xc = x32 - mu
    var = jnp.mean(xc * xc, axis=-1, keepdims=True)
    return (xc * lax.rsqrt(var + LN_EPS) * g + b).astype(x.dtype)


def group_layer_norm(t, w, groups):
    shp = t.shape
    t32 = t.astype(F32).reshape(*shp[:-1], groups, shp[-1] // groups)
    tc = t32 - jnp.mean(t32, axis=-1, keepdims=True)
    t32 = tc * lax.rsqrt(jnp.mean(tc * tc, axis=-1, keepdims=True) + LN_EPS)
    return (t32.reshape(shp) * w).astype(t.dtype)


def group_rms_norm(t, w, groups):
    shp = t.shape
    t32 = t.astype(F32).reshape(*shp[:-1], groups, shp[-1] // groups)
    t32 = t32 * lax.rsqrt(jnp.mean(t32 * t32, axis=-1, keepdims=True) + LN_EPS)
    return (t32.reshape(shp) * w).astype(t.dtype)


def rotary(t):
    s, dh = t.shape[1], t.shape[-1]
    inv_freq = ROPE_BASE ** (-jnp.arange(0, dh, 2, dtype=F32) / dh)
    ang = jnp.arange(s, dtype=F32)[:, None] * inv_freq[None, :]
    cos = jnp.cos(ang)[None, :, None, :]
    sin = jnp.sin(ang)[None, :, None, :]
    t1, t2 = jnp.split(t.astype(F32), 2, axis=-1)
    return jnp.concatenate([t1 * cos - t2 * sin, t1 * sin + t2 * cos], axis=-1).astype(t.dtype)


def _to_chunks(t):
    b, s = t.shape[:2]
    return jnp.moveaxis(t.reshape(b, s // CHUNK, CHUNK, *t.shape[2:]), 1, 0)


def _from_chunks(t):
    nc, b = t.shape[:2]
    return jnp.moveaxis(t, 0, 1).reshape(b, nc * CHUNK, *t.shape[3:])


def scalar_decay_scan(q, k, v, log_a):
    b, _, h, n = q.shape
    p = v.shape[-1]
    causal = jnp.tril(jnp.ones((CHUNK, CHUNK), dtype=bool))

    def step(state, inp):
        qc, kc, vc, ac = inp
        cum = jnp.cumsum(ac.astype(F32), axis=1)
        cum_h = jnp.swapaxes(cum, 1, 2)
        diff = cum_h[:, :, :, None] - cum_h[:, :, None, :]
        decay = jnp.exp(jnp.where(causal, diff, -jnp.inf))
        scores = jnp.einsum("bthn,bshn->bhts", qc, kc) * decay
        y = jnp.einsum("bhts,bshp->bthp", scores, vc)
        y = y + jnp.einsum("bthn,bhnp->bthp", qc * jnp.exp(cum)[..., None], state)
        last = cum[:, -1]
        k_dec = kc * jnp.exp(last[:, None, :] - cum)[..., None]
        state = jnp.exp(last)[:, :, None, None] * state + jnp.einsum("bshn,bshp->bhnp", k_dec, vc)
        return state, y

    state0 = jnp.zeros((b, h, n, p), F32)
    _, ys = lax.scan(step, state0, (_to_chunks(q), _to_chunks(k), _to_chunks(v), _to_chunks(log_a)))
    return _from_chunks(ys).astype(v.dtype)


def gated_decay_scan(q, k, v, log_f):
    b, _, h, dk = q.shape
    dv = v.shape[-1]
    causal = jnp.tril(jnp.ones((CHUNK, CHUNK), dtype=bool))[None, :, :, None, None]

    def step(state, inp):
        qc, kc, vc, fc = inp
        cum = jnp.cumsum(fc.astype(F32), axis=1)
        diff = cum[:, :, None] - cum[:, None, :]
        decay = jnp.exp(jnp.where(causal, diff, -jnp.inf))
        scores = jnp.einsum("bthk,bshk,btshk->bhts", qc, kc, decay)
        y = jnp.einsum("bhts,bshv->bthv", scores, vc)
        y = y + jnp.einsum("bthk,bhkv->bthv", qc * jnp.exp(cum), state)
        last = cum[:, -1]
        k_dec = kc * jnp.exp(last[:, None] - cum)
        state = jnp.exp(last)[..., None] * state + jnp.einsum("bshk,bshv->bhkv", k_dec, vc)
        return state, y

    state0 = jnp.zeros((b, h, dk, dv), F32)
    _, ys = lax.scan(step, state0, (_to_chunks(q), _to_chunks(k), _to_chunks(v), _to_chunks(log_f)))
    return _from_chunks(ys).astype(v.dtype)


def causal_depthwise_conv(x, w, bias):
    c = x.shape[-1]
    y = lax.conv_general_dilated(
        x, w[:, None, :], window_strides=(1,), padding=[(SSD_CONV - 1, 0)],
        dimension_numbers=("NWC", "WIO", "NWC"), feature_group_count=c)
    return y + bias


def retention_ssd_mixer(x, w_in, ret_norm_w, conv_w, conv_b, dt_bias, a_log, d_skip, ssd_norm_w, w_out):
    b, s, _ = x.shape
    q, k, v, g, z, xbc, dt_raw = _split(x @ w_in, EV_SIZES)
    q = rotary(q.reshape(b, s, RET_HEADS, RET_DK))
    k = rotary(k.reshape(b, s, RET_HEADS, RET_DK)) * (RET_DK ** -0.5)
    v = v.reshape(b, s, RET_HEADS, RET_DV)
    log_gamma = jnp.log1p(-jnp.exp2(-5.0 - jnp.arange(RET_HEADS, dtype=F32)))
    ret = scalar_decay_scan(q, k, v, jnp.broadcast_to(log_gamma, (b, s, RET_HEADS)))
    ret = group_layer_norm(ret.reshape(b, s, RET_V), ret_norm_w, RET_HEADS) * jax.nn.silu(g)
    xbc = jax.nn.silu(causal_depthwise_conv(xbc, conv_w, conv_b))
    xs, bm, cm = _split(xbc, (SSD_DI, SSD_G * SSD_N, SSD_G * SSD_N))
    xs = xs.reshape(b, s, SSD_HEADS, SSD_P)
    heads_per_group = SSD_HEADS // SSD_G
    bm = jnp.repeat(bm.reshape(b, s, SSD_G, SSD_N), heads_per_group, axis=2)
    cm = jnp.repeat(cm.reshape(b, s, SSD_G, SSD_N), heads_per_group, axis=2)
    dt = jax.nn.softplus(dt_raw.astype(F32) + dt_bias.astype(F32))
    log_a = -dt * jnp.exp(a_log.astype(F32))
    y = scalar_decay_scan(cm, bm, xs * dt[..., None].astype(xs.dtype), log_a)
    y = y + d_skip[:, None] * xs
    y = group_rms_norm(y.reshape(b, s, SSD_DI) * jax.nn.silu(z), ssd_norm_w, SSD_G)
    return jnp.concatenate([ret, y], axis=-1) @ w_out


def hgrn2_mixer(x, w_in, lower_bound, norm_w, w_out):
    b, s, _ = x.shape
    q, f_raw, i, g = _split(x @ w_in, OD_SIZES)
    lb = lower_bound.reshape(HG_HEADS, HG_DK)
    f_raw = f_raw.reshape(b, s, HG_HEADS, HG_DK).astype(F32)
    log_f = jnp.log(lb + (1.0 - lb) * jax.nn.sigmoid(f_raw))
    k = ((1.0 - lb) * jax.nn.sigmoid(-f_raw)).astype(x.dtype)
    q = jax.nn.silu(q.reshape(b, s, HG_HEADS, HG_DK))
    i = i.reshape(b, s, HG_HEADS, HG_DV)
    o = gated_decay_scan(q, k, i, log_f)
    o = group_rms_norm(o.reshape(b, s, HG_V), norm_w, HG_HEADS) * jax.nn.silu(g)
    return o @ w_out


def swiglu(x, w1, w3, w2):
    return (jax.nn.silu(x @ w1) * (x @ w3)) @ w2


def moe_swiglu(x, w_router, w1, w3, w2):
    b, s, d = x.shape
    xt = x.reshape(-1, d)
    n = xt.shape[0]
    logits = (xt @ w_router).astype(F32)
    top_logits, top_idx = lax.top_k(logits, TOP_K)
    gates = jax.nn.softmax(top_logits, axis=-1)
    n_assign = n * TOP_K
    expert_flat = top_idx.reshape(-1)
    token_flat = jnp.arange(n_assign, dtype=jnp.int32) // TOP_K
    gate_flat = gates.reshape(-1)
    order = jnp.argsort(expert_flat)
    sorted_expert = expert_flat[order]
    counts = jnp.zeros((N_EXPERTS,), jnp.int32).at[expert_flat].add(1)
    padded = (counts + MOE_BLOCK - 1) // MOE_BLOCK * MOE_BLOCK
    start_sorted = jnp.cumsum(counts) - counts
    padded_end = jnp.cumsum(padded)
    start_padded = padded_end - padded
    rank = jnp.arange(n_assign, dtype=jnp.int32) - start_sorted[sorted_expert]
    dest = start_padded[sorted_expert] + rank
    n_rows = n_assign + N_EXPERTS * MOE_BLOCK
    n_blocks = n_rows // MOE_BLOCK
    row_token = jnp.zeros((n_rows,), jnp.int32).at[dest].set(token_flat[order])
    row_gate = jnp.zeros((n_rows,), F32).at[dest].set(gate_flat[order])
    block_start = jnp.arange(n_blocks, dtype=jnp.int32) * MOE_BLOCK
    block_expert = jnp.minimum(jnp.searchsorted(padded_end, block_start, side="right"), N_EXPERTS - 1)
    xr = xt[row_token].reshape(n_blocks, MOE_BLOCK, d)

    def expert_block(args):
        xb, e = args
        return (jax.nn.silu(xb @ w1[e]) * (xb @ w3[e])) @ w2[e]

    yr = lax.map(expert_block, (xr, block_expert)).reshape(n_rows, d)
    out = jax.ops.segment_sum(yr * row_gate[:, None].astype(yr.dtype), row_token, num_segments=n)
    return out.reshape(b, s, d)


def setup_inputs(seed: int = 0) -> dict:
    key = jax.random.key(seed)
    k = jax.random.split(key, 30)
    ne, no = (DEPTH + 1) // 2, DEPTH // 2
    beta = (8.0 * DEPTH) ** -0.25

    def nrm(i, shape, scale):
        return scale * jax.random.normal(k[i], shape, F32)

    def gain(i, shape):
        return 1.0 + nrm(i, shape, 0.02)

    dt = jnp.exp(jax.random.uniform(k[5], (ne, SSD_HEADS), F32, math.log(1e-3), math.log(1e-1)))
    return {
        "x": nrm(0, (BATCH, SEQ, D_MODEL), 1.0),
        "ev_w_in": nrm(1, (ne, D_MODEL, EV_IN), D_MODEL ** -0.5),
        "ev_ret_norm_w": gain(2, (ne, RET_V)),
        "ev_conv_w": nrm(3, (ne, SSD_CONV, SSD_XBC), SSD_CONV ** -0.5),
        "ev_conv_b": nrm(4, (ne, SSD_XBC), 0.02),
        "ev_dt_bias": dt + jnp.log(-jnp.expm1(-dt)),
        "ev_a_log": jnp.log(jax.random.uniform(k[6], (ne, SSD_HEADS), F32, 1.0, 16.0)),
        "ev_d_skip": 1.0 + nrm(7, (ne, SSD_HEADS), 0.1),
        "ev_ssd_norm_w": gain(8, (ne, SSD_DI)),
        "ev_w_out": nrm(9, (ne, RET_V + SSD_DI, D_MODEL), beta * (RET_V + SSD_DI) ** -0.5),
        "ev_ln1_g": gain(10, (ne, D_MODEL)),
        "ev_ln1_b": nrm(11, (ne, D_MODEL), 0.02),
        "ffn_w1": nrm(12, (ne, D_MODEL, FFN_DIM), D_MODEL ** -0.5),
        "ffn_w3": nrm(13, (ne, D_MODEL, FFN_DIM), D_MODEL ** -0.5),
        "ffn_w2": nrm(14, (ne, FFN_DIM, D_MODEL), beta * FFN_DIM ** -0.5),
        "ev_ln2_g": gain(15, (ne, D_MODEL)),
        "ev_ln2_b": nrm(16, (ne, D_MODEL), 0.02),
        "od_w_in": nrm(17, (no, D_MODEL, OD_IN), D_MODEL ** -0.5),
        "hg_lb_logits": nrm(18, (DEPTH, HG_W), 0.5),
        "od_hg_norm_w": gain(19, (no, HG_V)),
        "od_w_out": nrm(20, (no, HG_V, D_MODEL), beta * HG_V ** -0.5),
        "od_ln1_g": gain(21, (no, D_MODEL)),
        "od_ln1_b": nrm(22, (no, D_MODEL), 0.02),
        "moe_router": nrm(23, (no, D_MODEL, N_EXPERTS), D_MODEL ** -0.5),
        "moe_w1": nrm(24, (no, N_EXPERTS, D_MODEL, FFN_DIM), D_MODEL ** -0.5),
        "moe_w3": nrm(25, (no, N_EXPERTS, D_MODEL, FFN_DIM), D_MODEL ** -0.5),
        "moe_w2": nrm(26, (no, N_EXPERTS, FFN_DIM, D_MODEL), beta * FFN_DIM ** -0.5),
        "od_ln2_g": gain(27, (no, D_MODEL)),
        "od_ln2_b": nrm(28, (no, D_MODEL), 0.02),
    }


def reference(x, ev_w_in, ev_ret_norm_w, ev_conv_w, ev_conv_b, ev_dt_bias, ev_a_log, ev_d_skip,
              ev_ssd_norm_w, ev_w_out, ev_ln1_g, ev_ln1_b, ffn_w1, ffn_w3, ffn_w2, ev_ln2_g, ev_ln2_b,
              od_w_in, hg_lb_logits, od_hg_norm_w, od_w_out, od_ln1_g, od_ln1_b, moe_router,
              moe_w1, moe_w3, moe_w2, od_ln2_g, od_ln2_b):
    alpha = (2.0 * DEPTH) ** 0.25
    lb_cum = jnp.cumsum(jax.nn.softmax(hg_lb_logits.astype(F32), axis=0), axis=0)
    lower_bounds = lb_cum - lb_cum[0]
    for layer in range(DEPTH):
        j = layer // 2
        if layer % 2 == 0:
            mix = retention_ssd_mixer(x, ev_w_in[j], ev_ret_norm_w[j], ev_conv_w[j], ev_conv_b[j],
                                      ev_dt_bias[j], ev_a_log[j], ev_d_skip[j], ev_ssd_norm_w[j],
                                      ev_w_out[j])
            x = layer_norm(alpha * x + mix, ev_ln1_g[j], ev_ln1_b[j])
            x = layer_norm(alpha * x + swiglu(x, ffn_w1[j], ffn_w3[j], ffn_w2[j]), ev_ln2_g[j], ev_ln2_b[j])
        else:
            mix = hgrn2_mixer(x, od_w_in[j], lower_bounds[layer], od_hg_norm_w[j], od_w_out[j])
            x = layer_norm(alpha * x + mix, od_ln1_g[j], od_ln1_b[j])
            ffn = moe_swiglu(x, moe_router[j], moe_w1[j], moe_w3[j], moe_w2[j])
            x = layer_norm(alpha * x + ffn, od_ln2_g[j], od_ln2_b[j])
    return x
```

```python
import functools
import math

import numpy as np
import jax
import jax.numpy as jnp
from jax import lax
from jax.experimental import pallas as pl
from jax.experimental.pallas import tpu as pltpu

F32 = jnp.float32
BF16 = jnp.bfloat16

D_MODEL = 2048
DEPTH = 4
LN_EPS = 1e-5
RET_HEADS = 8
RET_DK = 128
RET_DV = 256
RET_QK = RET_HEADS * RET_DK
RET_V = RET_HEADS * RET_DV
ROPE_BASE = 10000.0
SSD_DI = 2048
SSD_P = 64
SSD_HEADS = 32
SSD_G = 4
SSD_N = 128
SSD_CONV = 4
SSD_XBC = SSD_DI + 2 * SSD_G * SSD_N
SSD_HPG = SSD_HEADS // SSD_G
SSD_GW = SSD_HPG * SSD_P
HG_DK = 128
HG_HEADS = 16
FFN_DIM = 5632
N_EXPERTS = 8
EV_MAIN = 2 * RET_QK + 2 * RET_V + SSD_DI + SSD_XBC
ALPHA = (2.0 * DEPTH) ** 0.25

LANES = 128
SUBLANES = 8
VMEM_LIMIT = 52 * 1024 * 1024
NEG_BIG = -1e30

RET_CHUNK = 256
SSD_CHUNK = 256
HG_CHUNK = 64
HG_ROWS = 512
MOE_TM = 512
GATHER_ROWS = 256


def _cparams(*sem):
    return pltpu.CompilerParams(dimension_semantics=sem, vmem_limit_bytes=VMEM_LIMIT)


def _sigmoid(x):
    return 1.0 / (1.0 + jnp.exp(-x))


def _silu(x):
    return x * _sigmoid(x)


def _bdot(a, b):
    return jnp.dot(a.astype(BF16), b.astype(BF16), preferred_element_type=F32)


def _bdot_nt(a, b):
    return lax.dot_general(a.astype(BF16), b.astype(BF16), (((1,), (1,)), ((), ())),
                           preferred_element_type=F32)


def _bdot_tn(a, b):
    return lax.dot_general(a.astype(BF16), b.astype(BF16), (((0,), (0,)), ((), ())),
                           preferred_element_type=F32)


def _layer_norm_rows(y, g, b):
    mu = jnp.mean(y, axis=-1, keepdims=True)
    yc = y - mu
    var = jnp.mean(yc * yc, axis=-1, keepdims=True)
    return yc * lax.rsqrt(var + LN_EPS) * g + b


def _cumsum_rows(x):
    n = x.shape[0]
    row = lax.broadcasted_iota(jnp.int32, x.shape, 0)
    shift = 1
    while shift < n:
        x = x + jnp.where(row >= shift, pltpu.roll(x, shift, 0), 0.0)
        shift *= 2
    return x


def _mm_kernel(x_ref, w_ref, o_ref):
    o_ref[...] = jnp.dot(x_ref[...], w_ref[...], preferred_element_type=F32).astype(o_ref.dtype)


def _matmul(x, w, tm, tn, out_dtype):
    m, k = x.shape
    n = w.shape[1]
    return pl.pallas_call(
        _mm_kernel,
        grid=(m // tm, n // tn),
        in_specs=[pl.BlockSpec((tm, k), lambda i, j: (i, 0)),
                  pl.BlockSpec((k, tn), lambda i, j: (0, j))],
        out_specs=pl.BlockSpec((tm, tn), lambda i, j: (i, j)),
        out_shape=jax.ShapeDtypeStruct((m, n), out_dtype),
        compiler_params=_cparams("parallel", "parallel"),
    )(x, w)


def _mm_ln_kernel(*refs, n_in, kpi):
    a_refs = refs[:n_in]
    w_ref, r_ref, g_ref, b_ref, o32_ref, o16_ref, acc_ref = refs[n_in:]
    k = pl.program_id(1)
    for j in range(n_in):
        @pl.when((k >= j * kpi) & (k < (j + 1) * kpi))
        def _(j=j):
            p = jnp.dot(a_refs[j][...], w_ref[...], preferred_element_type=F32)

            @pl.when(k == 0)
            def _():
                acc_ref[...] = p

            @pl.when(k > 0)
            def _():
                acc_ref[...] += p

    @pl.when(k == n_in * kpi - 1)
    def _():
        y = ALPHA * r_ref[...] + acc_ref[...]
        out = _layer_norm_rows(y, g_ref[...], b_ref[...])
        o32_ref[...] = out
        o16_ref[...] = out.astype(BF16)


def _matmul_ln(a_list, w, resid, ln_g, ln_b, tm=512, tk=1024):
    n_in = len(a_list)
    m, ka = a_list[0].shape
    kpi = ka // tk
    n = w.shape[1]

    def a_map(j):
        return lambda i, k: (i, jnp.clip(k - j * kpi, 0, kpi - 1))

    in_specs = [pl.BlockSpec((tm, tk), a_map(j)) for j in range(n_in)]
    in_specs += [pl.BlockSpec((tk, n), lambda i, k: (k, 0)),
                 pl.BlockSpec((tm, n), lambda i, k: (i, 0)),
                 pl.BlockSpec((1, n), lambda i, k: (0, 0)),
                 pl.BlockSpec((1, n), lambda i, k: (0, 0))]
    return pl.pallas_call(
        functools.partial(_mm_ln_kernel, n_in=n_in, kpi=kpi),
        grid=(m // tm, n_in * kpi),
        in_specs=in_specs,
        out_specs=[pl.BlockSpec((tm, n), lambda i, k: (i, 0)),
                   pl.BlockSpec((tm, n), lambda i, k: (i, 0))],
        out_shape=[jax.ShapeDtypeStruct((m, n), F32), jax.ShapeDtypeStruct((m, n), BF16)],
        scratch_shapes=[pltpu.VMEM((tm, n), F32)],
        compiler_params=_cparams("parallel", "arbitrary"),
    )(*a_list, w, resid, ln_g.reshape(1, n), ln_b.reshape(1, n))


def _ffn_kernel(be_ref, bv_ref, x_ref, w1_ref, w3_ref, w2_ref, *rest, with_ln):
    if with_ln:
        r_ref, g_ref, b_ref, o32_ref, o16_ref = rest
    else:
        (o32_ref,) = rest
    i = pl.program_id(0)
    j = pl.program_id(1)
    nj = pl.num_programs(1)
    valid = bv_ref[i] > 0

    @pl.when(valid)
    def _():
        x = x_ref[...]
        h1 = jnp.dot(x, w1_ref[...], preferred_element_type=F32)
        h3 = jnp.dot(x, w3_ref[...], preferred_element_type=F32)
        h = (_silu(h1) * h3).astype(BF16)
        p = jnp.dot(h, w2_ref[...], preferred_element_type=F32)

        @pl.when(j == 0)
        def _():
            o32_ref[...] = p

        @pl.when(j > 0)
        def _():
            o32_ref[...] += p

    @pl.when(jnp.logical_not(valid) & (j == 0))
    def _():
        o32_ref[...] = jnp.zeros_like(o32_ref)

    if with_ln:
        @pl.when(j == nj - 1)
        def _():
            y = ALPHA * r_ref[...] + o32_ref[...]
            out = _layer_norm_rows(y, g_ref[...], b_ref[...])
            o32_ref[...] = out
            o16_ref[...] = out.astype(BF16)


def _ffn(x16, w1, w3, w2, blk_expert, blk_valid, tm, tf=512, ln=None):
    rows, d = x16.shape
    f = w1.shape[2]
    nj = f // tf
    with_ln = ln is not None

    def jeff(i, j, bv):
        return jnp.where(bv[i] > 0, j, nj - 1)

    in_specs = [pl.BlockSpec((tm, d), lambda i, j, be, bv: (i, 0)),
                pl.BlockSpec((None, d, tf), lambda i, j, be, bv: (be[i], 0, jeff(i, j, bv))),
                pl.BlockSpec((None, d, tf), lambda i, j, be, bv: (be[i], 0, jeff(i, j, bv))),
                pl.BlockSpec((None, tf, d), lambda i, j, be, bv: (be[i], jeff(i, j, bv), 0))]
    args = [x16, w1, w3, w2]
    out_specs = [pl.BlockSpec((tm, d), lambda i, j, be, bv: (i, 0))]
    out_shape = [jax.ShapeDtypeStruct((rows, d), F32)]
    if with_ln:
        resid, g, b = ln
        in_specs += [pl.BlockSpec((tm, d), lambda i, j, be, bv: (i, 0)),
                     pl.BlockSpec((1, d), lambda i, j, be, bv: (0, 0)),
                     pl.BlockSpec((1, d), lambda i, j, be, bv: (0, 0))]
        args += [resid, g.reshape(1, d), b.reshape(1, d)]
        out_specs.append(pl.BlockSpec((tm, d), lambda i, j, be, bv: (i, 0)))
        out_shape.append(jax.ShapeDtypeStruct((rows, d), BF16))
    res = pl.pallas_call(
        functools.partial(_ffn_kernel, with_ln=with_ln),
        grid_spec=pltpu.PrefetchScalarGridSpec(
            num_scalar_prefetch=2, grid=(rows // tm, nj),
            in_specs=in_specs, out_specs=out_specs),
        out_shape=out_shape,
        compiler_params=_cparams("parallel", "arbitrary"),
    )(blk_expert, blk_valid, *args)
    return res if with_ln else res[0]


def _retention_kernel(lg_ref, q_ref, k_ref, v_ref, g_ref, cos_ref, sin_ref, nw_ref, o_ref, state_ref):
    c = RET_CHUNK
    h = pl.program_id(1)

    @pl.when(pl.program_id(2) == 0)
    def _():
        state_ref[...] = jnp.zeros_like(state_ref)

    lg = lg_ref[h]
    cos = cos_ref[...]
    sin = sin_ref[...]
    q = q_ref[...]
    k = k_ref[...]
    qr = q * cos + pltpu.roll(q, RET_DK // 2, 1) * sin
    kr = (k * cos + pltpu.roll(k, RET_DK // 2, 1) * sin) * (RET_DK ** -0.5)
    t = lax.broadcasted_iota(jnp.int32, (c, 1), 0).astype(F32)
    tt = lax.broadcasted_iota(jnp.int32, (c, c), 0)
    ss = lax.broadcasted_iota(jnp.int32, (c, c), 1)
    decay = jnp.exp(jnp.where(tt >= ss, (tt - ss).astype(F32) * lg, NEG_BIG))
    v = v_ref[...].astype(BF16)
    state = state_ref[...]
    scores = _bdot_nt(qr, kr) * decay
    y = jnp.dot(scores.astype(BF16), v, preferred_element_type=F32)
    y = y + _bdot(qr * jnp.exp((t + 1.0) * lg), state)
    k_dec = kr * jnp.exp((float(c - 1) - t) * lg)
    state_ref[...] = jnp.exp(float(c) * lg) * state + _bdot_tn(k_dec, v)
    mu = jnp.mean(y, axis=-1, keepdims=True)
    yc = y - mu
    var = jnp.mean(yc * yc, axis=-1, keepdims=True)
    yn = yc * lax.rsqrt(var + LN_EPS) * nw_ref[...]
    o_ref[...] = (yn * _silu(g_ref[...])).astype(o_ref.dtype)


def _retention(proj, log_gamma, cos_t, sin_t, norm_w):
    b, s, _ = proj.shape
    c = RET_CHUNK
    kb = RET_QK // RET_DK
    vb = 2 * RET_QK // RET_DV
    gb = vb + RET_V // RET_DV
    return pl.pallas_call(
        _retention_kernel,
        grid_spec=pltpu.PrefetchScalarGridSpec(
            num_scalar_prefetch=1, grid=(b, RET_HEADS, s // c),
            in_specs=[pl.BlockSpec((None, c, RET_DK), lambda bi, h, ci, lg: (bi, ci, h)),
                      pl.BlockSpec((None, c, RET_DK), lambda bi, h, ci, lg: (bi, ci, kb + h)),
                      pl.BlockSpec((None, c, RET_DV), lambda bi, h, ci, lg: (bi, ci, vb + h)),
                      pl.BlockSpec((None, c, RET_DV), lambda bi, h, ci, lg: (bi, ci, gb + h)),
                      pl.BlockSpec((c, RET_DK), lambda bi, h, ci, lg: (ci, 0)),
                      pl.BlockSpec((c, RET_DK), lambda bi, h, ci, lg: (ci, 0)),
                      pl.BlockSpec((1, RET_DV), lambda bi, h, ci, lg: (0, h))],
            out_specs=pl.BlockSpec((None, c, RET_DV), lambda bi, h, ci, lg: (bi, ci, h)),
            scratch_shapes=[pltpu.VMEM((RET_DK, RET_DV), F32)]),
        out_shape=jax.ShapeDtypeStruct((b, s, RET_V), BF16),
        compiler_params=_cparams("parallel", "parallel", "arbitrary"),
    )(log_gamma, proj, proj, proj, proj, cos_t, sin_t, norm_w.reshape(1, RET_V))


def _softplus(x):
    return jnp.maximum(x, 0.0) + jnp.log1p(jnp.exp(-jnp.abs(x)))


def _ssd_kernel(z_ref, xs_ref, bm_ref, cm_ref, dt_ref, cwx_ref, cwb_ref, cwc_ref,
                cbx_ref, cbb_ref, cbc_ref, dtb_ref, alog_ref, dsk_ref, nw_ref,
                o_ref, state_ref, ext_ref):
    c = SSD_CHUNK
    gw = SSD_GW
    n = SSD_N
    pad = SUBLANES

    @pl.when(pl.program_id(2) == 0)
    def _():
        state_ref[...] = jnp.zeros_like(state_ref)
        ext_ref[0:pad, :] = jnp.zeros((pad, gw + 2 * n), F32)

    ext_ref[pad:pad + c, 0:gw] = xs_ref[...]
    ext_ref[pad:pad + c, gw:gw + n] = bm_ref[...]
    ext_ref[pad:pad + c, gw + n:gw + 2 * n] = cm_ref[...]
    cw = jnp.concatenate([cwx_ref[...], cwb_ref[...], cwc_ref[...]], axis=1)
    cb = jnp.concatenate([cbx_ref[...], cbb_ref[...], cbc_ref[...]], axis=1)
    conv = cb + cw[SSD_CONV - 1:SSD_CONV, :] * ext_ref[pad:pad + c, :]
    for j in range(SSD_CONV - 1):
        off = pad - (SSD_CONV - 1) + j
        conv = conv + cw[j:j + 1, :] * ext_ref[off:off + c, :]
    tail = ext_ref[c:c + pad, :]
    ext_ref[0:pad, :] = tail
    conv = _silu(conv)
    xs = conv[:, 0:gw]
    bm = conv[:, gw:gw + n]
    cm = conv[:, gw + n:gw + 2 * n]

    dt = _softplus(dt_ref[...] + dtb_ref[...])
    log_a = -dt * jnp.exp(alog_ref[...])
    cum = _cumsum_rows(log_a)
    cum_t = cum.T
    last = cum[c - 1:c, :]
    e_cum = jnp.exp(cum)
    e_rem = jnp.exp(last - cum)
    e_last = jnp.exp(last)

    tt = lax.broadcasted_iota(jnp.int32, (c, c), 0)
    ss = lax.broadcasted_iota(jnp.int32, (c, c), 1)
    causal = tt >= ss
    lane = lax.broadcasted_iota(jnp.int32, (c, 2 * SSD_P), 1)
    first = lane < SSD_P
    lane1 = lax.broadcasted_iota(jnp.int32, (1, 2 * SSD_P), 1)
    first1 = lane1 < SSD_P

    state = state_ref[...]
    gmat = _bdot_nt(cm, bm)
    cross = _bdot(cm, state)
    ys, xws, els = [], [], []
    for p in range(SSD_HPG // 2):
        ha, hb = 2 * p, 2 * p + 1
        lo, hi = p * 2 * SSD_P, (p + 1) * 2 * SSD_P
        xs_p = xs[:, lo:hi]
        xdt = xs_p * jnp.where(first, dt[:, ha:ha + 1], dt[:, hb:hb + 1])
        xdt16 = xdt.astype(BF16)
        dec_a = jnp.exp(jnp.where(causal, cum[:, ha:ha + 1] - cum_t[ha:ha + 1, :], NEG_BIG))
        dec_b = jnp.exp(jnp.where(causal, cum[:, hb:hb + 1] - cum_t[hb:hb + 1, :], NEG_BIG))
        ya = jnp.dot((gmat * dec_a).astype(BF16), xdt16, preferred_element_type=F32)
        yb = jnp.dot((gmat * dec_b).astype(BF16), xdt16, preferred_element_type=F32)
        y = jnp.where(first, ya, yb)
        y = y + cross[:, lo:hi] * jnp.where(first, e_cum[:, ha:ha + 1], e_cum[:, hb:hb + 1])
        y = y + dsk_ref[:, lo:hi] * xs_p
        ys.append(y)
        xws.append(xdt * jnp.where(first, e_rem[:, ha:ha + 1], e_rem[:, hb:hb + 1]))
        els.append(jnp.where(first1, e_last[:, ha:ha + 1], e_last[:, hb:hb + 1]))
    y = jnp.concatenate(ys, axis=1)
    xw = jnp.concatenate(xws, axis=1)
    el = jnp.concatenate(els, axis=1)
    state_ref[...] = state * el + _bdot_tn(bm, xw)

    yz = y * _silu(z_ref[...])
    ms = jnp.mean(yz * yz, axis=-1, keepdims=True)
    o_ref[...] = (yz * lax.rsqrt(ms + LN_EPS) * nw_ref[...]).astype(o_ref.dtype)


def _ssd(proj, dt_raw, conv_w, conv_b, dtb, alog, dskip, norm_w):
    b, s, _ = proj.shape
    c = SSD_CHUNK
    gw = SSD_GW
    n = SSD_N
    z0 = (2 * RET_QK + 2 * RET_V) // gw
    x0 = z0 + SSD_DI // gw
    b0 = (2 * RET_QK + 2 * RET_V + SSD_DI + SSD_DI) // n
    c0 = b0 + SSD_G
    wb0 = SSD_DI // n
    wc0 = wb0 + SSD_G
    im3 = lambda f: (lambda bi, g, ci: f(bi, g, ci))
    return pl.pallas_call(
        _ssd_kernel,
        grid=(b, SSD_G, s // c),
        in_specs=[pl.BlockSpec((None, c, gw), im3(lambda bi, g, ci: (bi, ci, z0 + g))),
                  pl.BlockSpec((None, c, gw), im3(lambda bi, g, ci: (bi, ci, x0 + g))),
                  pl.BlockSpec((None, c, n), im3(lambda bi, g, ci: (bi, ci, b0 + g))),
                  pl.BlockSpec((None, c, n), im3(lambda bi, g, ci: (bi, ci, c0 + g))),
                  pl.BlockSpec((None, c, LANES), im3(lambda bi, g, ci: (bi, ci, g))),
                  pl.BlockSpec((SSD_CONV, gw), im3(lambda bi, g, ci: (0, g))),
                  pl.BlockSpec((SSD_CONV, n), im3(lambda bi, g, ci: (0, wb0 + g))),
                  pl.BlockSpec((SSD_CONV, n), im3(lambda bi, g, ci: (0, wc0 + g))),
                  pl.BlockSpec((1, gw), im3(lambda bi, g, ci: (0, g))),
                  pl.BlockSpec((1, n), im3(lambda bi, g, ci: (0, wb0 + g))),
                  pl.BlockSpec((1, n), im3(lambda bi, g, ci: (0, wc0 + g))),
                  pl.BlockSpec((1, LANES), im3(lambda bi, g, ci: (0, g))),
                  pl.BlockSpec((1, LANES), im3(lambda bi, g, ci: (0, g))),
                  pl.BlockSpec((1, gw), im3(lambda bi, g, ci: (0, g))),
                  pl.BlockSpec((1, gw), im3(lambda bi, g, ci: (0, g)))],
        out_specs=pl.BlockSpec((None, c, gw), im3(lambda bi, g, ci: (bi, ci, g))),
        out_shape=jax.ShapeDtypeStruct((b, s, SSD_DI), BF16),
        scratch_shapes=[pltpu.VMEM((n, gw), F32), pltpu.VMEM((c + SUBLANES, gw + 2 * n), F32)],
        compiler_params=_cparams("parallel", "parallel", "arbitrary"),
    )(proj, proj, proj, proj, dt_raw, conv_w, conv_w, conv_w,
      conv_b.reshape(1, SSD_XBC), conv_b.reshape(1, SSD_XBC), conv_b.reshape(1, SSD_XBC),
      dtb, alog, dskip, norm_w.reshape(1, SSD_DI))


def _group_lanes(v):
    v = v.reshape(SSD_G, SSD_HPG)
    return jnp.pad(v, ((0, 0), (0, LANES - SSD_HPG))).reshape(1, SSD_G * LANES)


def _hgrn2_masks():
    c = HG_CHUNK
    t = lax.broadcasted_iota(jnp.int32, (c, c), 0)
    s = lax.broadcasted_iota(jnp.int32, (c, c), 1)
    masks = []
    blk = c
    while blk > SUBLANES:
        half = blk // 2
        m = (t // blk == s // blk) & ((t % blk) >= half) & ((s % blk) < half)
        masks.append((blk, m))
        blk = half
    return masks


def _hgrn2_kernel(q_ref, f_ref, i_ref, g_ref, lbl_ref, nw_ref, o_ref, state_ref, *, layer):
    c = HG_CHUNK
    dk = HG_DK

    @pl.when(pl.program_id(2) == 0)
    def _():
        state_ref[...] = jnp.zeros_like(state_ref)

    lbl = lbl_ref[...]
    ex = jnp.exp(lbl - jnp.max(lbl, axis=0, keepdims=True))
    pr = ex / jnp.sum(ex, axis=0, keepdims=True)
    lb = jnp.sum(pr[1:layer + 1, :], axis=0, keepdims=True)
    nw = nw_ref[...]

    masks = _hgrn2_masks()
    row = lax.broadcasted_iota(jnp.int32, (c, dk), 0)
    sub = row % SUBLANES
    trow = lax.broadcasted_iota(jnp.int32, (c, c), 0)
    scol = lax.broadcasted_iota(jnp.int32, (c, c), 1)
    tile_base = (trow // SUBLANES) * SUBLANES

    def sub_bcast(x, j):
        x3 = x.reshape(c // SUBLANES, SUBLANES, dk)
        return jnp.broadcast_to(x3[:, j:j + 1, :], x3.shape).reshape(c, dk)

    def chunk(ci, carry):
        r0 = pl.multiple_of(ci * c, c)
        fr = f_ref[pl.ds(r0, c), :]
        sg = _sigmoid(fr)
        log_f = jnp.log(lb + (1.0 - lb) * sg)
        kk = (1.0 - lb) * _sigmoid(-fr)
        qq = _silu(q_ref[pl.ds(r0, c), :])
        v16 = i_ref[pl.ds(r0, c), :].astype(BF16)
        cum = _cumsum_rows(log_f)

        scores = jnp.zeros((c, c), F32)
        for blk, m in masks:
            half = blk // 2
            refv = jnp.concatenate(
                [jnp.broadcast_to(cum[nb * blk + half - 1:nb * blk + half, :], (blk, dk))
                 for nb in range(c // blk)], axis=0)
            upper = (row % blk) >= half
            qe = qq * jnp.exp(jnp.where(upper, cum - refv, NEG_BIG))
            ke = kk * jnp.exp(jnp.where(upper, NEG_BIG, refv - cum))
            scores = scores + jnp.where(m, _bdot_nt(qe, ke), 0.0)
        for j in range(SUBLANES):
            e = jnp.exp(jnp.where(sub >= j, cum - sub_bcast(cum, j), NEG_BIG))
            col = jnp.sum(qq * e * sub_bcast(kk, j), axis=-1, keepdims=True)
            scores = scores + jnp.where(scol == tile_base + j, col, 0.0)

        state_t = state_ref[...]
        y = jnp.dot(scores.astype(BF16), v16, preferred_element_type=F32)
        y = y + _bdot_nt(qq * jnp.exp(cum), state_t)
        last = cum[c - 1:c, :]
        k_dec = kk * jnp.exp(last - cum)
        state_ref[...] = state_t * jnp.exp(last) + _bdot_tn(v16, k_dec)
        ms = jnp.mean(y * y, axis=-1, keepdims=True)
        out = y * lax.rsqrt(ms + LN_EPS) * nw * _silu(g_ref[pl.ds(r0, c), :])
        o_ref[pl.ds(r0, c), :] = out.astype(o_ref.dtype)
        return carry

    lax.fori_loop(0, HG_ROWS // c, chunk, 0)


def _hgrn2(proj, lb_logits, norm_w, layer):
    b, s, _ = proj.shape
    r = HG_ROWS
    h = HG_HEADS
    return pl.pallas_call(
        functools.partial(_hgrn2_kernel, layer=layer),
        grid=(b, h, s // r),
        in_specs=[pl.BlockSpec((None, r, HG_DK), lambda bi, hi, ri: (bi, ri, hi)),
                  pl.BlockSpec((None, r, HG_DK), lambda bi, hi, ri: (bi, ri, h + hi)),
                  pl.BlockSpec((None, r, HG_DK), lambda bi, hi, ri: (bi, ri, 2 * h + hi)),
                  pl.BlockSpec((None, r, HG_DK), lambda bi, hi, ri: (bi, ri, 3 * h + hi)),
                  pl.BlockSpec((DEPTH, HG_DK), lambda bi, hi, ri: (0, hi)),
                  pl.BlockSpec((1, HG_DK), lambda bi, hi, ri: (0, hi))],
        out_specs=pl.BlockSpec((None, r, HG_DK), lambda bi, hi, ri: (bi, ri, hi)),
        out_shape=jax.ShapeDtypeStruct((b, s, h * HG_DK), BF16),
        scratch_shapes=[pltpu.VMEM((HG_DK, HG_DK), F32)],
        compiler_params=_cparams("parallel", "parallel", "arbitrary"),
    )(proj, proj, proj, proj, lb_logits, norm_w.reshape(1, h * HG_DK))


def _router_kernel(x_ref, w_ref, idx_ref, gate_ref, cnt_ref, carry_ref):
    tm = x_ref.shape[0]

    @pl.when(pl.program_id(0) == 0)
    def _():
        carry_ref[...] = jnp.zeros_like(carry_ref)

    logits = jnp.dot(x_ref[...], w_ref[...], preferred_element_type=F32,
                     precision=lax.Precision.HIGHEST)
    lane = lax.broadcasted_iota(jnp.int32, (tm, LANES), 1)
    lane_f = lane.astype(F32)
    lg = jnp.where(lane < N_EXPERTS, logits, NEG_BIG)
    m1 = jnp.max(lg, axis=-1, keepdims=True)
    i1 = jnp.min(jnp.where(lg == m1, lane_f, float(LANES)), axis=-1, keepdims=True).astype(jnp.int32)
    lg2 = jnp.where(lane == i1, NEG_BIG, lg)
    m2 = jnp.max(lg2, axis=-1, keepdims=True)
    i2 = jnp.min(jnp.where(lg2 == m2, lane_f, float(LANES)), axis=-1, keepdims=True).astype(jnp.int32)
    e2 = jnp.exp(m2 - m1)
    den = 1.0 + e2
    gate_ref[...] = jnp.where(lane == 0, 1.0 / den, jnp.where(lane == 1, e2 / den, 0.0))

    sel1 = lane == i1
    sel2 = lane == i2
    assign = (sel1 | sel2).astype(BF16)
    tt = lax.broadcasted_iota(jnp.int32, (tm, tm), 0)
    ss = lax.broadcasted_iota(jnp.int32, (tm, tm), 1)
    before = (ss < tt).astype(BF16)
    base = carry_ref[...] + jnp.dot(before, assign, preferred_element_type=F32)
    r1 = jnp.sum(jnp.where(sel1, base, 0.0), axis=-1, keepdims=True).astype(jnp.int32)
    r2 = jnp.sum(jnp.where(sel2, base, 0.0), axis=-1, keepdims=True).astype(jnp.int32)
    idx_ref[...] = jnp.where(lane == 0, i1, jnp.where(lane == 1, i2,
                             jnp.where(lane == 2, r1, jnp.where(lane == 3, r2, 0))))
    carry_ref[...] += jnp.sum(assign.astype(F32), axis=0, keepdims=True)
    cnt_ref[...] = carry_ref[...]


def _router(x32, w_router, tm=512):
    n, d = x32.shape
    wr = jnp.pad(w_router, ((0, 0), (0, LANES - N_EXPERTS)))
    return pl.pallas_call(
        _router_kernel,
        grid=(n // tm,),
        in_specs=[pl.BlockSpec((tm, d), lambda i: (i, 0)),
                  pl.BlockSpec((d, LANES), lambda i: (0, 0))],
        out_specs=[pl.BlockSpec((tm, LANES), lambda i: (i, 0)),
                   pl.BlockSpec((tm, LANES), lambda i: (i, 0)),
                   pl.BlockSpec((1, LANES), lambda i: (0, 0))],
        out_shape=[jax.ShapeDtypeStruct((n, LANES), jnp.int32),
                   jax.ShapeDtypeStruct((n, LANES), F32),
                   jax.ShapeDtypeStruct((1, LANES), F32)],
        scratch_shapes=[pltpu.VMEM((1, LANES), F32)],
        compiler_params=_cparams("arbitrary"),
    )(x32, wr)


def _row_copy(src_hbm, dst_vmem, src_row, dst_row, sem):
    return pltpu.make_async_copy(src_hbm.at[pl.ds(src_row, 1), :], dst_vmem.at[pl.ds(dst_row, 1), :], sem)


def _gather_kernel(tok_ref, x_hbm, o_ref, buf_ref, sem):
    tg = o_ref.shape[0]
    base = pl.program_id(0) * tg

    def start(r, carry):
        _row_copy(x_hbm, buf_ref, tok_ref[base + r], r, sem).start()
        return carry

    def wait(r, carry):
        _row_copy(x_hbm, buf_ref, tok_ref[base + r], r, sem).wait()
        return carry

    lax.fori_loop(0, tg, start, 0)
    lax.fori_loop(0, tg, wait, 0)
    o_ref[...] = buf_ref[...].astype(o_ref.dtype)


def _gather_rows(x32, row_token, tg=GATHER_ROWS):
    n_rows = row_token.shape[0]
    d = x32.shape[1]
    return pl.pallas_call(
        _gather_kernel,
        grid_spec=pltpu.PrefetchScalarGridSpec(
            num_scalar_prefetch=1, grid=(n_rows // tg,),
            in_specs=[pl.BlockSpec(memory_space=pl.ANY)],
            out_specs=pl.BlockSpec((tg, d), lambda i, tok: (i, 0)),
            scratch_shapes=[pltpu.VMEM((tg, d), F32), pltpu.SemaphoreType.DMA(())]),
        out_shape=jax.ShapeDtypeStruct((n_rows, d), BF16),
        compiler_params=_cparams("arbitrary"),
    )(row_token, x32)


def _combine_kernel(pos_ref, y_hbm, gate_ref, x_ref, g_ref, b_ref, o32_ref, o16_ref, buf0, buf1, sem):
    tc = x_ref.shape[0]
    base = pl.program_id(0) * tc

    def start(r, carry):
        _row_copy(y_hbm, buf0, pos_ref[2 * (base + r)], r, sem.at[0]).start()
        _row_copy(y_hbm, buf1, pos_ref[2 * (base + r) + 1], r, sem.at[1]).start()
        return carry

    def wait(r, carry):
        _row_copy(y_hbm, buf0, pos_ref[2 * (base + r)], r, sem.at[0]).wait()
        _row_copy(y_hbm, buf1, pos_ref[2 * (base + r) + 1], r, sem.at[1]).wait()
        return carry

    lax.fori_loop(0, tc, start, 0)
    lax.fori_loop(0, tc, wait, 0)
    gates = gate_ref[...]
    ffn = gates[:, 0:1] * buf0[...] + gates[:, 1:2] * buf1[...]
    out = _layer_norm_rows(ALPHA * x_ref[...] + ffn, g_ref[...], b_ref[...])
    o32_ref[...] = out
    o16_ref[...] = out.astype(BF16)


def _combine(pos, yr, gates, x32, ln_g, ln_b, tc=GATHER_ROWS):
    n, d = x32.shape
    return pl.pallas_call(
        _combine_kernel,
        grid_spec=pltpu.PrefetchScalarGridSpec(
            num_scalar_prefetch=1, grid=(n // tc,),
            in_specs=[pl.BlockSpec(memory_space=pl.ANY),
                      pl.BlockSpec((tc, LANES), lambda i, p: (i, 0)),
                      pl.BlockSpec((tc, d), lambda i, p: (i, 0)),
                      pl.BlockSpec((1, d), lambda i, p: (0, 0)),
                      pl.BlockSpec((1, d), lambda i, p: (0, 0))],
            out_specs=[pl.BlockSpec((tc, d), lambda i, p: (i, 0)),
                       pl.BlockSpec((tc, d), lambda i, p: (i, 0))],
            scratch_shapes=[pltpu.VMEM((tc, d), F32), pltpu.VMEM((tc, d), F32),
                            pltpu.SemaphoreType.DMA((2,))]),
        out_shape=[jax.ShapeDtypeStruct((n, d), F32), jax.ShapeDtypeStruct((n, d), BF16)],
        compiler_params=_cparams("arbitrary"),
    )(pos, yr, gates, x32, ln_g.reshape(1, d), ln_b.reshape(1, d))


def _moe(x32, w_router, w1, w3, w2, ln_g, ln_b):
    n, d = x32.shape
    tm = MOE_TM
    idx, gates, counts = _router(x32, w_router)
    counts = counts[0, :N_EXPERTS].astype(jnp.int32)
    expert = idx[:, 0:2]
    rank = idx[:, 2:4]
    padded = (counts + tm - 1) // tm * tm
    padded_end = jnp.cumsum(padded)
    start_padded = padded_end - padded
    dest = start_padded[expert] + rank
    n_rows = 2 * n + N_EXPERTS * tm
    n_blocks = n_rows // tm
    token = jnp.broadcast_to(jnp.arange(n, dtype=jnp.int32)[:, None], (n, 2))
    row_token = jnp.zeros((n_rows,), jnp.int32).at[dest.reshape(-1)].set(token.reshape(-1))
    block_start = jnp.arange(n_blocks, dtype=jnp.int32) * tm
    blk_valid = (block_start < padded_end[-1]).astype(jnp.int32)
    last_expert = jnp.minimum(jnp.searchsorted(padded_end, padded_end[-1] - 1, side="right"), N_EXPERTS - 1)
    blk_expert = jnp.minimum(jnp.searchsorted(padded_end, block_start, side="right"), N_EXPERTS - 1)
    blk_expert = jnp.where(blk_valid > 0, blk_expert, last_expert).astype(jnp.int32)
    xr = _gather_rows(x32, row_token)
    yr = _ffn(xr, w1, w3, w2, blk_expert, blk_valid, tm)
    return _combine(dest.reshape(-1).astype(jnp.int32), yr, gates, x32, ln_g, ln_b)


def _rope_tables(s):
    inv_freq = ROPE_BASE ** (-jnp.arange(0, RET_DK, 2, dtype=F32) / RET_DK)
    ang = jnp.arange(s, dtype=F32)[:, None] * inv_freq[None, :]
    cos = jnp.cos(ang)
    sin = jnp.sin(ang)
    return jnp.concatenate([cos, cos], axis=-1), jnp.concatenate([-sin, sin], axis=-1)


def _even_layer(x32, x16, b, s, w_in, ret_norm_w, conv_w, conv_b, dt_bias, a_log, d_skip,
                ssd_norm_w, w_out, ln1_g, ln1_b, w1, w3, w2, ln2_g, ln2_b):
    n = b * s
    w_main = w_in[:, :EV_MAIN].astype(BF16)
    w_dt = w_in[:, EV_MAIN:].reshape(D_MODEL, SSD_G, SSD_HPG)
    w_dt = jnp.pad(w_dt, ((0, 0), (0, 0), (0, LANES - SSD_HPG))).reshape(D_MODEL, SSD_G * LANES).astype(BF16)
    proj = _matmul(x16, w_main, 1024, 1024, F32).reshape(b, s, EV_MAIN)
    dt_raw = _matmul(x16, w_dt, 1024, SSD_G * LANES, F32).reshape(b, s, SSD_G * LANES)
    log_gamma = jnp.log1p(-jnp.exp2(-5.0 - jnp.arange(RET_HEADS, dtype=F32)))
    cos_t, sin_t = _rope_tables(s)
    ret = _retention(proj, log_gamma, cos_t, sin_t, ret_norm_w)
    ssd = _ssd(proj, dt_raw, conv_w, conv_b, _group_lanes(dt_bias), _group_lanes(a_log),
               jnp.repeat(d_skip, SSD_P).reshape(1, SSD_DI), ssd_norm_w)
    x32, x16 = _matmul_ln([ret.reshape(n, RET_V), ssd.reshape(n, SSD_DI)], w_out.astype(BF16),
                          x32, ln1_g, ln1_b)
    ones = jnp.ones((n // 512,), jnp.int32)
    return _ffn(x16, w1.astype(BF16)[None], w3.astype(BF16)[None], w2.astype(BF16)[None],
                jnp.zeros((n // 512,), jnp.int32), ones, 512, ln=(x32, ln2_g, ln2_b))


def _odd_layer(x32, x16, b, s, layer, w_in, lb_logits, hg_norm_w, w_out, ln1_g, ln1_b,
               w_router, w1, w3, w2, ln2_g, ln2_b):
    n = b * s
    proj = _matmul(x16, w_in.astype(BF16), 1024, 1024, F32).reshape(b, s, -1)
    o = _hgrn2(proj, lb_logits, hg_norm_w, layer)
    x32, x16 = _matmul_ln([o.reshape(n, -1)], w_out.astype(BF16), x32, ln1_g, ln1_b)
    return _moe(x32, w_router, w1.astype(BF16), w3.astype(BF16), w2.astype(BF16), ln2_g, ln2_b)


def kernel(x, ev_w_in, ev_ret_norm_w, ev_conv_w, ev_conv_b, ev_dt_bias, ev_a_log, ev_d_skip,
           ev_ssd_norm_w, ev_w_out, ev_ln1_g, ev_ln1_b, ffn_w1, ffn_w3, ffn_w2, ev_ln2_g, ev_ln2_b,
           od_w_in, hg_lb_logits, od_hg_norm_w, od_w_out, od_ln1_g, od_ln1_b, moe_router,
           moe_w1, moe_w3, moe_w2, od_ln2_g, od_ln2_b):
    b, s, d = x.shape
    x32 = x.reshape(b * s, d)
    x16 = x32.astype(BF16)
    for layer in range(DEPTH):
        j = layer // 2
        if layer % 2 == 0:
            x32, x16 = _even_layer(x32, x16, b, s, ev_w_in[j], ev_ret_norm_w[j], ev_conv_w[j], ev_conv_b[j],
                                   ev_dt_bias[j], ev_a_log[j], ev_d_skip[j], ev_ssd_norm_w[j], ev_w_out[j],
                                   ev_ln1_g[j], ev_ln1_b[j], ffn_w1[j], ffn_w3[j], ffn_w2[j],
                                   ev_ln2_g[j], ev_ln2_b[j])
        else:
            x32, x16 = _odd_layer(x32, x16, b, s, layer, od_w_in[j], hg_lb_logits, od_hg_norm_w[j],
                                  od_w_out[j], od_ln1_g[j], od_ln1_b[j], moe_router[j],
                                  moe_w1[j], moe_w3[j], moe_w2[j], od_ln2_g[j], od_ln2_b[j])
    return x32.reshape(b, s, d)
```

```python
import functools
import math

import numpy as np
import jax
import jax.numpy as jnp
from jax import lax
from jax.experimental import pallas as pl
from jax.experimental.pallas import tpu as pltpu

F32 = jnp.float32
BF16 = jnp.bfloat16

D_MODEL = 2048
DEPTH = 4
LN_EPS = 1e-5
RET_HEADS = 8
RET_DK = 128
RET_DV = 256
RET_QK = RET_HEADS * RET_DK
RET_V = RET_HEADS * RET_DV
ROPE_BASE = 10000.0
SSD_DI = 2048
SSD_P = 64
SSD_HEADS = 32
SSD_G = 4
SSD_N = 128
SSD_CONV = 4
SSD_XBC = SSD_DI + 2 * SSD_G * SSD_N
SSD_HPG = SSD_HEADS // SSD_G
SSD_GW = SSD_HPG * SSD_P
HG_DK = 128
HG_HEADS = 16
FFN_DIM = 5632
N_EXPERTS = 8
EV_MAIN = 2 * RET_QK + 2 * RET_V + SSD_DI + SSD_XBC
ALPHA = (2.0 * DEPTH) ** 0.25

LANES = 128
SUBLANES = 8
VMEM_LIMIT = 52 * 1024 * 1024
NEG_BIG = -1e30

RET_CHUNK = 256
SSD_CHUNK = 256
HG_CHUNK = 64
HG_ROWS = 512
HG_HPS = 8
MOE_TM = 512
GATHER_ROWS = 256


def _cparams(*sem):
    return pltpu.CompilerParams(dimension_semantics=sem, vmem_limit_bytes=VMEM_LIMIT)


def _sigmoid(x):
    return 0.5 * jnp.tanh(0.5 * x) + 0.5


def _silu(x):
    return x * _sigmoid(x)


def _bdot(a, b):
    return jnp.dot(a.astype(BF16), b.astype(BF16), preferred_element_type=F32)


def _bdot_nt(a, b):
    return lax.dot_general(a.astype(BF16), b.astype(BF16), (((1,), (1,)), ((), ())),
                           preferred_element_type=F32)


def _bdot_tn(a, b):
    return lax.dot_general(a.astype(BF16), b.astype(BF16), (((0,), (0,)), ((), ())),
                           preferred_element_type=F32)


def _layer_norm_rows(y, g, b):
    mu = jnp.mean(y, axis=-1, keepdims=True)
    yc = y - mu
    var = jnp.mean(yc * yc, axis=-1, keepdims=True)
    return yc * lax.rsqrt(var + LN_EPS) * g + b


def _cumsum_rows(x):
    n = x.shape[0]
    row = lax.broadcasted_iota(jnp.int32, x.shape, 0)
    shift = 1
    while shift < n:
        x = x + jnp.where(row >= shift, pltpu.roll(x, shift, 0), 0.0)
        shift *= 2
    return x


def _mm_kernel(x_ref, w_ref, o_ref):
    o_ref[...] = jnp.dot(x_ref[...], w_ref[...], preferred_element_type=F32).astype(o_ref.dtype)


def _matmul(x, w, tm, tn, out_dtype, n=None):
    m, k = x.shape
    n = w.shape[1] if n is None else n
    return pl.pallas_call(
        _mm_kernel,
        grid=(m // tm, n // tn),
        in_specs=[pl.BlockSpec((tm, k), lambda i, j: (i, 0)),
                  pl.BlockSpec((k, tn), lambda i, j: (0, j))],
        out_specs=pl.BlockSpec((tm, tn), lambda i, j: (i, j)),
        out_shape=jax.ShapeDtypeStruct((m, n), out_dtype),
        compiler_params=_cparams("parallel", "parallel"),
    )(x, w)


def _mm_ln_kernel(*refs, n_in, kpi):
    a_refs = refs[:n_in]
    w_ref, r_ref, g_ref, b_ref, o32_ref, o16_ref, acc_ref = refs[n_in:]
    k = pl.program_id(1)
    for j in range(n_in):
        @pl.when((k >= j * kpi) & (k < (j + 1) * kpi))
        def _(j=j):
            p = jnp.dot(a_refs[j][...], w_ref[...], preferred_element_type=F32)

            @pl.when(k == 0)
            def _():
                acc_ref[...] = p

            @pl.when(k > 0)
            def _():
                acc_ref[...] += p

    @pl.when(k == n_in * kpi - 1)
    def _():
        y = ALPHA * r_ref[...] + acc_ref[...]
        out = _layer_norm_rows(y, g_ref[...], b_ref[...])
        o32_ref[...] = out
        o16_ref[...] = out.astype(BF16)


def _matmul_ln(a_list, w, resid, ln_g, ln_b, tm=512, tk=1024):
    n_in = len(a_list)
    m, ka = a_list[0].shape
    kpi = ka // tk
    n = w.shape[1]

    def a_map(j):
        return lambda i, k: (i, jnp.clip(k - j * kpi, 0, kpi - 1))

    in_specs = [pl.BlockSpec((tm, tk), a_map(j)) for j in range(n_in)]
    in_specs += [pl.BlockSpec((tk, n), lambda i, k: (k, 0)),
                 pl.BlockSpec((tm, n), lambda i, k: (i, 0)),
                 pl.BlockSpec((1, n), lambda i, k: (0, 0)),
                 pl.BlockSpec((1, n), lambda i, k: (0, 0))]
    return pl.pallas_call(
        functools.partial(_mm_ln_kernel, n_in=n_in, kpi=kpi),
        grid=(m // tm, n_in * kpi),
        in_specs=in_specs,
        out_specs=[pl.BlockSpec((tm, n), lambda i, k: (i, 0)),
                   pl.BlockSpec((tm, n), lambda i, k: (i, 0))],
        out_shape=[jax.ShapeDtypeStruct((m, n), F32), jax.ShapeDtypeStruct((m, n), BF16)],
        scratch_shapes=[pltpu.VMEM((tm, n), F32)],
        compiler_params=_cparams("parallel", "arbitrary"),
    )(*a_list, w, resid, ln_g.reshape(1, n), ln_b.reshape(1, n))


def _swiglu_accumulate(x16, w1_ref, w3_ref, w2_ref, o32_ref, j):
    h1 = jnp.dot(x16, w1_ref[...], preferred_element_type=F32)
    h3 = jnp.dot(x16, w3_ref[...], preferred_element_type=F32)
    h = (_silu(h1) * h3).astype(BF16)
    p = jnp.dot(h, w2_ref[...], preferred_element_type=F32)

    @pl.when(j == 0)
    def _():
        o32_ref[...] = p

    @pl.when(j > 0)
    def _():
        o32_ref[...] += p


def _ffn_ln_kernel(x_ref, w1_ref, w3_ref, w2_ref, r_ref, g_ref, b_ref, o32_ref, o16_ref):
    j = pl.program_id(1)
    _swiglu_accumulate(x_ref[...], w1_ref, w3_ref, w2_ref, o32_ref, j)

    @pl.when(j == pl.num_programs(1) - 1)
    def _():
        y = ALPHA * r_ref[...] + o32_ref[...]
        out = _layer_norm_rows(y, g_ref[...], b_ref[...])
        o32_ref[...] = out
        o16_ref[...] = out.astype(BF16)


def _ffn_ln(x16, w1, w3, w2, resid, ln_g, ln_b, tm=512, tf=512):
    rows, d = x16.shape
    f = w1.shape[1]
    row_blk = pl.BlockSpec((tm, d), lambda i, j: (i, 0))
    vec = pl.BlockSpec((1, d), lambda i, j: (0, 0))
    return pl.pallas_call(
        _ffn_ln_kernel,
        grid=(rows // tm, f // tf),
        in_specs=[row_blk,
                  pl.BlockSpec((d, tf), lambda i, j: (0, j)),
                  pl.BlockSpec((d, tf), lambda i, j: (0, j)),
                  pl.BlockSpec((tf, d), lambda i, j: (j, 0)),
                  row_blk, vec, vec],
        out_specs=[row_blk, row_blk],
        out_shape=[jax.ShapeDtypeStruct((rows, d), F32), jax.ShapeDtypeStruct((rows, d), BF16)],
        compiler_params=_cparams("parallel", "arbitrary"),
    )(x16, w1, w3, w2, resid, ln_g.reshape(1, d), ln_b.reshape(1, d))


def _row_copy(src_hbm, dst_vmem, src_row, dst_row, sem):
    return pltpu.make_async_copy(src_hbm.at[pl.ds(src_row, 1), :], dst_vmem.at[pl.ds(dst_row, 1), :], sem)


def _moe_ffn_kernel(be_ref, bv_ref, tok_ref, x_hbm, w1_ref, w3_ref, w2_ref, o32_ref, buf_ref, x16_ref, sem):
    i = pl.program_id(0)
    j = pl.program_id(1)
    nblk = pl.num_programs(0)
    tm = o32_ref.shape[0]
    valid = bv_ref[i] > 0
    slot = i % 2

    def gather(blk, slt, wait):
        def body(r, carry):
            cp = _row_copy(x_hbm, buf_ref.at[slt], tok_ref[blk * tm + r], r, sem.at[slt])
            if wait:
                cp.wait()
            else:
                cp.start()
            return carry
        lax.fori_loop(0, tm, body, 0, unroll=4)

    @pl.when((j == 0) & (i == 0) & valid)
    def _():
        gather(0, 0, wait=False)

    @pl.when((j == 0) & (i + 1 < nblk))
    def _():
        @pl.when(bv_ref[i + 1] > 0)
        def _():
            gather(i + 1, 1 - slot, wait=False)

    @pl.when((j == 0) & valid)
    def _():
        gather(i, slot, wait=True)
        x16_ref[...] = buf_ref[slot].astype(BF16)

    @pl.when(valid)
    def _():
        _swiglu_accumulate(x16_ref[...], w1_ref, w3_ref, w2_ref, o32_ref, j)

    @pl.when(jnp.logical_not(valid) & (j == 0))
    def _():
        o32_ref[...] = jnp.zeros_like(o32_ref)


def _moe_ffn(x32, row_token, w1, w3, w2, blk_expert, blk_valid, tm, tf=512):
    n_rows = row_token.shape[0]
    d = x32.shape[1]
    nj = w1.shape[2] // tf

    def jeff(i, j, bv):
        return jnp.where(bv[i] > 0, j, nj - 1)

    return pl.pallas_call(
        _moe_ffn_kernel,
        grid_spec=pltpu.PrefetchScalarGridSpec(
            num_scalar_prefetch=3, grid=(n_rows // tm, nj),
            in_specs=[pl.BlockSpec(memory_space=pl.ANY),
                      pl.BlockSpec((None, d, tf), lambda i, j, be, bv, tok: (be[i], 0, jeff(i, j, bv))),
                      pl.BlockSpec((None, d, tf), lambda i, j, be, bv, tok: (be[i], 0, jeff(i, j, bv))),
                      pl.BlockSpec((None, tf, d), lambda i, j, be, bv, tok: (be[i], jeff(i, j, bv), 0))],
            out_specs=pl.BlockSpec((tm, d), lambda i, j, be, bv, tok: (i, 0)),
            scratch_shapes=[pltpu.VMEM((2, tm, d), F32), pltpu.VMEM((tm, d), BF16),
                            pltpu.SemaphoreType.DMA((2,))]),
        out_shape=jax.ShapeDtypeStruct((n_rows, d), F32),
        compiler_params=_cparams("arbitrary", "arbitrary"),
    )(blk_expert, blk_valid, row_token, x32, w1, w3, w2)


def _retention_kernel(lg_ref, q_ref, k_ref, v_ref, g_ref, cos_ref, sin_ref, nw_ref, o_ref, state_ref):
    c = RET_CHUNK
    h = pl.program_id(1)

    @pl.when(pl.program_id(2) == 0)
    def _():
        state_ref[...] = jnp.zeros_like(state_ref)

    lg = lg_ref[h]
    cos = cos_ref[...]
    sin = sin_ref[...]
    q = q_ref[...]
    k = k_ref[...]
    qr = q * cos + pltpu.roll(q, RET_DK // 2, 1) * sin
    kr = (k * cos + pltpu.roll(k, RET_DK // 2, 1) * sin) * (RET_DK ** -0.5)
    t = lax.broadcasted_iota(jnp.int32, (c, 1), 0).astype(F32)
    tt = lax.broadcasted_iota(jnp.int32, (c, c), 0)
    ss = lax.broadcasted_iota(jnp.int32, (c, c), 1)
    decay = jnp.exp(jnp.where(tt >= ss, (tt - ss).astype(F32) * lg, NEG_BIG))
    v = v_ref[...].astype(BF16)
    state = state_ref[...]
    scores = _bdot_nt(qr, kr) * decay
    y = jnp.dot(scores.astype(BF16), v, preferred_element_type=F32)
    y = y + _bdot(qr * jnp.exp((t + 1.0) * lg), state)
    k_dec = kr * jnp.exp((float(c - 1) - t) * lg)
    state_ref[...] = jnp.exp(float(c) * lg) * state + _bdot_tn(k_dec, v)
    mu = jnp.mean(y, axis=-1, keepdims=True)
    yc = y - mu
    var = jnp.mean(yc * yc, axis=-1, keepdims=True)
    yn = yc * lax.rsqrt(var + LN_EPS) * nw_ref[...]
    o_ref[...] = (yn * _silu(g_ref[...])).astype(o_ref.dtype)


def _retention(proj, log_gamma, cos_t, sin_t, norm_w):
    b, s, _ = proj.shape
    c = RET_CHUNK
    kb = RET_QK // RET_DK
    vb = 2 * RET_QK // RET_DV
    gb = vb + RET_V // RET_DV
    return pl.pallas_call(
        _retention_kernel,
        grid_spec=pltpu.PrefetchScalarGridSpec(
            num_scalar_prefetch=1, grid=(b, RET_HEADS, s // c),
            in_specs=[pl.BlockSpec((None, c, RET_DK), lambda bi, h, ci, lg: (bi, ci, h)),
                      pl.BlockSpec((None, c, RET_DK), lambda bi, h, ci, lg: (bi, ci, kb + h)),
                      pl.BlockSpec((None, c, RET_DV), lambda bi, h, ci, lg: (bi, ci, vb + h)),
                      pl.BlockSpec((None, c, RET_DV), lambda bi, h, ci, lg: (bi, ci, gb + h)),
                      pl.BlockSpec((c, RET_DK), lambda bi, h, ci, lg: (ci, 0)),
                      pl.BlockSpec((c, RET_DK), lambda bi, h, ci, lg: (ci, 0)),
                      pl.BlockSpec((1, RET_DV), lambda bi, h, ci, lg: (0, h))],
            out_specs=pl.BlockSpec((None, c, RET_DV), lambda bi, h, ci, lg: (bi, ci, h)),
            scratch_shapes=[pltpu.VMEM((RET_DK, RET_DV), F32)]),
        out_shape=jax.ShapeDtypeStruct((b, s, RET_V), BF16),
        compiler_params=_cparams("parallel", "parallel", "arbitrary"),
    )(log_gamma, proj, proj, proj, proj, cos_t, sin_t, norm_w.reshape(1, RET_V))


def _softplus(x):
    return jnp.maximum(x, 0.0) + jnp.log1p(jnp.exp(-jnp.abs(x)))


def _ssd_kernel(z_ref, xs_ref, bm_ref, cm_ref, dt_ref, cwx_ref, cwb_ref, cwc_ref,
                cbx_ref, cbb_ref, cbc_ref, dtb_ref, alog_ref, dsk_ref, nw_ref,
                o_ref, state_ref, ext_ref):
    c = SSD_CHUNK
    gw = SSD_GW
    n = SSD_N
    pad = SUBLANES

    @pl.when(pl.program_id(2) == 0)
    def _():
        state_ref[...] = jnp.zeros_like(state_ref)
        ext_ref[0:pad, :] = jnp.zeros((pad, gw + 2 * n), F32)

    ext_ref[pad:pad + c, 0:gw] = xs_ref[...]
    ext_ref[pad:pad + c, gw:gw + n] = bm_ref[...]
    ext_ref[pad:pad + c, gw + n:gw + 2 * n] = cm_ref[...]
    cw = jnp.concatenate([cwx_ref[...], cwb_ref[...], cwc_ref[...]], axis=1)
    cb = jnp.concatenate([cbx_ref[...], cbb_ref[...], cbc_ref[...]], axis=1)
    conv = cb + cw[SSD_CONV - 1:SSD_CONV, :] * ext_ref[pad:pad + c, :]
    for j in range(SSD_CONV - 1):
        off = pad - (SSD_CONV - 1) + j
        conv = conv + cw[j:j + 1, :] * ext_ref[off:off + c, :]
    tail = ext_ref[c:c + pad, :]
    ext_ref[0:pad, :] = tail
    conv = _silu(conv)
    xs = conv[:, 0:gw]
    bm = conv[:, gw:gw + n]
    cm = conv[:, gw + n:gw + 2 * n]

    dt = _softplus(dt_ref[...] + dtb_ref[...])
    log_a = -dt * jnp.exp(alog_ref[...])
    cum = _cumsum_rows(log_a)
    cum_t = cum.T
    last = cum[c - 1:c, :]
    e_cum = jnp.exp(cum)
    e_rem = jnp.exp(last - cum)
    e_last = jnp.exp(last)

    tt = lax.broadcasted_iota(jnp.int32, (c, c), 0)
    ss = lax.broadcasted_iota(jnp.int32, (c, c), 1)
    causal = tt >= ss
    lane = lax.broadcasted_iota(jnp.int32, (c, 2 * SSD_P), 1)
    first = lane < SSD_P
    lane1 = lax.broadcasted_iota(jnp.int32, (1, 2 * SSD_P), 1)
    first1 = lane1 < SSD_P

    state = state_ref[...]
    gmat = _bdot_nt(cm, bm)
    cross = _bdot(cm, state)
    ys, xws, els = [], [], []
    for p in range(SSD_HPG // 2):
        ha, hb = 2 * p, 2 * p + 1
        lo, hi = p * 2 * SSD_P, (p + 1) * 2 * SSD_P
        xs_p = xs[:, lo:hi]
        xdt = xs_p * jnp.where(first, dt[:, ha:ha + 1], dt[:, hb:hb + 1])
        xdt16 = xdt.astype(BF16)
        dec_a = jnp.exp(jnp.where(causal, cum[:, ha:ha + 1] - cum_t[ha:ha + 1, :], NEG_BIG))
        dec_b = jnp.exp(jnp.where(causal, cum[:, hb:hb + 1] - cum_t[hb:hb + 1, :], NEG_BIG))
        ya = jnp.dot((gmat * dec_a).astype(BF16), xdt16, preferred_element_type=F32)
        yb = jnp.dot((gmat * dec_b).astype(BF16), xdt16, preferred_element_type=F32)
        y = jnp.where(first, ya, yb)
        y = y + cross[:, lo:hi] * jnp.where(first, e_cum[:, ha:ha + 1], e_cum[:, hb:hb + 1])
        y = y + dsk_ref[:, lo:hi] * xs_p
        ys.append(y)
        xws.append(xdt * jnp.where(first, e_rem[:, ha:ha + 1], e_rem[:, hb:hb + 1]))
        els.append(jnp.where(first1, e_last[:, ha:ha + 1], e_last[:, hb:hb + 1]))
    y = jnp.concatenate(ys, axis=1)
    xw = jnp.concatenate(xws, axis=1)
    el = jnp.concatenate(els, axis=1)
    state_ref[...] = state * el + _bdot_tn(bm, xw)

    yz = y * _silu(z_ref[...])
    ms = jnp.mean(yz * yz, axis=-1, keepdims=True)
    o_ref[...] = (yz * lax.rsqrt(ms + LN_EPS) * nw_ref[...]).astype(o_ref.dtype)


def _ssd(proj, dt_raw, conv_w, conv_b, dtb, alog, dskip, norm_w):
    b, s, _ = proj.shape
    c = SSD_CHUNK
    gw = SSD_GW
    n = SSD_N
    z0 = (2 * RET_QK + 2 * RET_V) // gw
    x0 = z0 + SSD_DI // gw
    b0 = (2 * RET_QK + 2 * RET_V + SSD_DI + SSD_DI) // n
    c0 = b0 + SSD_G
    wb0 = SSD_DI // n
    wc0 = wb0 + SSD_G
    im3 = lambda f: (lambda bi, g, ci: f(bi, g, ci))
    return pl.pallas_call(
        _ssd_kernel,
        grid=(b, SSD_G, s // c),
        in_specs=[pl.BlockSpec((None, c, gw), im3(lambda bi, g, ci: (bi, ci, z0 + g))),
                  pl.BlockSpec((None, c, gw), im3(lambda bi, g, ci: (bi, ci, x0 + g))),
                  pl.BlockSpec((None, c, n), im3(lambda bi, g, ci: (bi, ci, b0 + g))),
                  pl.BlockSpec((None, c, n), im3(lambda bi, g, ci: (bi, ci, c0 + g))),
                  pl.BlockSpec((None, c, LANES), im3(lambda bi, g, ci: (bi, ci, g))),
                  pl.BlockSpec((SSD_CONV, gw), im3(lambda bi, g, ci: (0, g))),
                  pl.BlockSpec((SSD_CONV, n), im3(lambda bi, g, ci: (0, wb0 + g))),
                  pl.BlockSpec((SSD_CONV, n), im3(lambda bi, g, ci: (0, wc0 + g))),
                  pl.BlockSpec((1, gw), im3(lambda bi, g, ci: (0, g))),
                  pl.BlockSpec((1, n), im3(lambda bi, g, ci: (0, wb0 + g))),
                  pl.BlockSpec((1, n), im3(lambda bi, g, ci: (0, wc0 + g))),
                  pl.BlockSpec((1, LANES), im3(lambda bi, g, ci: (0, g))),
                  pl.BlockSpec((1, LANES), im3(lambda bi, g, ci: (0, g))),
                  pl.BlockSpec((1, gw), im3(lambda bi, g, ci: (0, g))),
                  pl.BlockSpec((1, gw), im3(lambda bi, g, ci: (0, g)))],
        out_specs=pl.BlockSpec((None, c, gw), im3(lambda bi, g, ci: (bi, ci, g))),
        out_shape=jax.ShapeDtypeStruct((b, s, SSD_DI), BF16),
        scratch_shapes=[pltpu.VMEM((n, gw), F32), pltpu.VMEM((c + SUBLANES, gw + 2 * n), F32)],
        compiler_params=_cparams("parallel", "parallel", "arbitrary"),
    )(proj, proj, proj, proj, dt_raw, conv_w, conv_w, conv_w,
      conv_b.reshape(1, SSD_XBC), conv_b.reshape(1, SSD_XBC), conv_b.reshape(1, SSD_XBC),
      dtb, alog, dskip, norm_w.reshape(1, SSD_DI))


def _group_lanes(v):
    v = v.reshape(SSD_G, SSD_HPG)
    return jnp.pad(v, ((0, 0), (0, LANES - SSD_HPG))).reshape(1, SSD_G * LANES)


def _hgrn2_masks():
    c = HG_CHUNK
    t = lax.broadcasted_iota(jnp.int32, (c, c), 0)
    s = lax.broadcasted_iota(jnp.int32, (c, c), 1)
    masks = []
    blk = c
    while blk > SUBLANES:
        half = blk // 2
        m = (t // blk == s // blk) & ((t % blk) >= half) & ((s % blk) < half)
        masks.append((blk, m))
        blk = half
    return masks


def _hgrn2_kernel(q_ref, f_ref, i_ref, g_ref, lbl_ref, nw_ref, o_ref, state_ref, *, layer):
    c = HG_CHUNK
    dk = HG_DK

    @pl.when(pl.program_id(2) == 0)
    def _():
        state_ref[...] = jnp.zeros_like(state_ref)

    lbl = lbl_ref[...]
    ex = jnp.exp(lbl - jnp.max(lbl, axis=0, keepdims=True))
    pr = ex / jnp.sum(ex, axis=0, keepdims=True)
    lb_all = jnp.sum(pr[1:layer + 1, :], axis=0, keepdims=True)
    nw_all = nw_ref[...]

    masks = _hgrn2_masks()
    row = lax.broadcasted_iota(jnp.int32, (c, dk), 0)
    sub = row % SUBLANES
    trow = lax.broadcasted_iota(jnp.int32, (c, c), 0)
    scol = lax.broadcasted_iota(jnp.int32, (c, c), 1)
    tile_base = (trow // SUBLANES) * SUBLANES

    def sub_bcast(x, j):
        x3 = x.reshape(c // SUBLANES, SUBLANES, dk)
        return jnp.broadcast_to(x3[:, j:j + 1, :], x3.shape).reshape(c, dk)

    def head_chunk(r0, hh):
        cols = slice(hh * dk, (hh + 1) * dk)
        lb = lb_all[:, cols]
        fr = f_ref[pl.ds(r0, c), cols]
        sg = _sigmoid(fr)
        log2_f = jnp.log2(lb + (1.0 - lb) * sg)
        kk = (1.0 - lb) * (1.0 - sg)
        qq = _silu(q_ref[pl.ds(r0, c), cols])
        v16 = i_ref[pl.ds(r0, c), cols].astype(BF16)
        cum = _cumsum_rows(log2_f)

        scores = jnp.zeros((c, c), F32)
        for blk, m in masks:
            half = blk // 2
            refv = jnp.concatenate(
                [jnp.broadcast_to(cum[nb * blk + half - 1:nb * blk + half, :], (blk, dk))
                 for nb in range(c // blk)], axis=0)
            e = jnp.exp2(-jnp.abs(cum - refv))
            scores = scores + jnp.where(m, _bdot_nt(qq * e, kk * e), 0.0)
        for j in range(SUBLANES):
            e = jnp.exp2(jnp.where(sub >= j, cum - sub_bcast(cum, j), NEG_BIG))
            col = jnp.sum(qq * e * sub_bcast(kk, j), axis=-1, keepdims=True)
            scores = scores + jnp.where(scol == tile_base + j, col, 0.0)

        state_t = state_ref[hh]
        y = jnp.dot(scores.astype(BF16), v16, preferred_element_type=F32)
        y = y + _bdot_nt(qq * jnp.exp2(cum), state_t)
        last = cum[c - 1:c, :]
        k_dec = kk * jnp.exp2(last - cum)
        state_ref[hh] = state_t * jnp.exp2(last) + _bdot_tn(v16, k_dec)
        ms = jnp.mean(y * y, axis=-1, keepdims=True)
        out = y * lax.rsqrt(ms + LN_EPS) * nw_all[:, cols] * _silu(g_ref[pl.ds(r0, c), cols])
        o_ref[pl.ds(r0, c), cols] = out.astype(o_ref.dtype)

    def chunk(ci, carry):
        r0 = pl.multiple_of(ci * c, c)
        for hh in range(HG_HPS):
            head_chunk(r0, hh)
        return carry

    lax.fori_loop(0, HG_ROWS // c, chunk, 0)


def _hgrn2(proj, lb_logits, norm_w, layer):
    b, s, _ = proj.shape
    r = HG_ROWS
    w = HG_HPS * HG_DK
    nb = HG_HEADS // HG_HPS
    return pl.pallas_call(
        functools.partial(_hgrn2_kernel, layer=layer),
        grid=(b, nb, s // r),
        in_specs=[pl.BlockSpec((None, r, w), lambda bi, hi, ri: (bi, ri, hi)),
                  pl.BlockSpec((None, r, w), lambda bi, hi, ri: (bi, ri, nb + hi)),
                  pl.BlockSpec((None, r, w), lambda bi, hi, ri: (bi, ri, 2 * nb + hi)),
                  pl.BlockSpec((None, r, w), lambda bi, hi, ri: (bi, ri, 3 * nb + hi)),
                  pl.BlockSpec((DEPTH, w), lambda bi, hi, ri: (0, hi)),
                  pl.BlockSpec((1, w), lambda bi, hi, ri: (0, hi))],
        out_specs=pl.BlockSpec((None, r, w), lambda bi, hi, ri: (bi, ri, hi)),
        out_shape=jax.ShapeDtypeStruct((b, s, HG_HEADS * HG_DK), BF16),
        scratch_shapes=[pltpu.VMEM((HG_HPS, HG_DK, HG_DK), F32)],
        compiler_params=_cparams("parallel", "parallel", "arbitrary"),
    )(proj, proj, proj, proj, lb_logits, norm_w.reshape(1, HG_HEADS * HG_DK))


def _router_kernel(x_ref, w_ref, idx_ref, gate_ref, cnt_ref, carry_ref):
    tm = x_ref.shape[0]

    @pl.when(pl.program_id(0) == 0)
    def _():
        carry_ref[...] = jnp.zeros_like(carry_ref)

    logits = jnp.dot(x_ref[...], w_ref[...], preferred_element_type=F32,
                     precision=lax.Precision.HIGHEST)
    lane = lax.broadcasted_iota(jnp.int32, (tm, LANES), 1)
    lane_f = lane.astype(F32)
    lg = jnp.where(lane < N_EXPERTS, logits, NEG_BIG)
    m1 = jnp.max(lg, axis=-1, keepdims=True)
    i1 = jnp.min(jnp.where(lg == m1, lane_f, float(LANES)), axis=-1, keepdims=True).astype(jnp.int32)
    lg2 = jnp.where(lane == i1, NEG_BIG, lg)
    m2 = jnp.max(lg2, axis=-1, keepdims=True)
    i2 = jnp.min(jnp.where(lg2 == m2, lane_f, float(LANES)), axis=-1, keepdims=True).astype(jnp.int32)
    e2 = jnp.exp(m2 - m1)
    den = 1.0 + e2
    gate_ref[...] = jnp.where(lane == 0, 1.0 / den, jnp.where(lane == 1, e2 / den, 0.0))

    sel1 = lane == i1
    sel2 = lane == i2
    assign = (sel1 | sel2).astype(BF16)
    tt = lax.broadcasted_iota(jnp.int32, (tm, tm), 0)
    ss = lax.broadcasted_iota(jnp.int32, (tm, tm), 1)
    before = (ss < tt).astype(BF16)
    base = carry_ref[...] + jnp.dot(before, assign, preferred_element_type=F32)
    r1 = jnp.sum(jnp.where(sel1, base, 0.0), axis=-1, keepdims=True).astype(jnp.int32)
    r2 = jnp.sum(jnp.where(sel2, base, 0.0), axis=-1, keepdims=True).astype(jnp.int32)
    idx_ref[...] = jnp.where(lane == 0, i1, jnp.where(lane == 1, i2,
                             jnp.where(lane == 2, r1, jnp.where(lane == 3, r2, 0))))
    carry_ref[...] += jnp.sum(assign.astype(F32), axis=0, keepdims=True)
    cnt_ref[...] = carry_ref[...]


def _router(x32, w_router, tm=512):
    n, d = x32.shape
    wr = jnp.pad(w_router, ((0, 0), (0, LANES - N_EXPERTS)))
    return pl.pallas_call(
        _router_kernel,
        grid=(n // tm,),
        in_specs=[pl.BlockSpec((tm, d), lambda i: (i, 0)),
                  pl.BlockSpec((d, LANES), lambda i: (0, 0))],
        out_specs=[pl.BlockSpec((tm, LANES), lambda i: (i, 0)),
                   pl.BlockSpec((tm, LANES), lambda i: (i, 0)),
                   pl.BlockSpec((1, LANES), lambda i: (0, 0))],
        out_shape=[jax.ShapeDtypeStruct((n, LANES), jnp.int32),
                   jax.ShapeDtypeStruct((n, LANES), F32),
                   jax.ShapeDtypeStruct((1, LANES), F32)],
        scratch_shapes=[pltpu.VMEM((1, LANES), F32)],
        compiler_params=_cparams("arbitrary"),
    )(x32, wr)


def _combine_kernel(pos_ref, y_hbm, gate_ref, x_ref, g_ref, b_ref, o32_ref, o16_ref, buf_ref, sem):
    i = pl.program_id(0)
    nblk = pl.num_programs(0)
    tc = x_ref.shape[0]
    slot = i % 2

    def gather(blk, slt, wait):
        def body(r, carry):
            for k in range(2):
                cp = _row_copy(y_hbm, buf_ref.at[slt, k], pos_ref[2 * (blk * tc + r) + k], r, sem.at[slt])
                if wait:
                    cp.wait()
                else:
                    cp.start()
            return carry
        lax.fori_loop(0, tc, body, 0, unroll=2)

    @pl.when(i == 0)
    def _():
        gather(0, 0, wait=False)

    @pl.when(i + 1 < nblk)
    def _():
        gather(i + 1, 1 - slot, wait=False)

    gather(i, slot, wait=True)
    gates = gate_ref[...]
    ffn = gates[:, 0:1] * buf_ref[slot, 0] + gates[:, 1:2] * buf_ref[slot, 1]
    out = _layer_norm_rows(ALPHA * x_ref[...] + ffn, g_ref[...], b_ref[...])
    o32_ref[...] = out
    o16_ref[...] = out.astype(BF16)


def _combine(pos, yr, gates, x32, ln_g, ln_b, tc=GATHER_ROWS):
    n, d = x32.shape
    return pl.pallas_call(
        _combine_kernel,
        grid_spec=pltpu.PrefetchScalarGridSpec(
            num_scalar_prefetch=1, grid=(n // tc,),
            in_specs=[pl.BlockSpec(memory_space=pl.ANY),
                      pl.BlockSpec((tc, LANES), lambda i, p: (i, 0)),
                      pl.BlockSpec((tc, d), lambda i, p: (i, 0)),
                      pl.BlockSpec((1, d), lambda i, p: (0, 0)),
                      pl.BlockSpec((1, d), lambda i, p: (0, 0))],
            out_specs=[pl.BlockSpec((tc, d), lambda i, p: (i, 0)),
                       pl.BlockSpec((tc, d), lambda i, p: (i, 0))],
            scratch_shapes=[pltpu.VMEM((2, 2, tc, d), F32), pltpu.SemaphoreType.DMA((2,))]),
        out_shape=[jax.ShapeDtypeStruct((n, d), F32), jax.ShapeDtypeStruct((n, d), BF16)],
        compiler_params=_cparams("arbitrary"),
    )(pos, yr, gates, x32, ln_g.reshape(1, d), ln_b.reshape(1, d))


def _moe(x32, w_router, w1, w3, w2, ln_g, ln_b):
    n, d = x32.shape
    tm = MOE_TM
    idx, gates, counts = _router(x32, w_router)
    counts = counts[0, :N_EXPERTS].astype(jnp.int32)
    expert = idx[:, 0:2]
    rank = idx[:, 2:4]
    padded = (counts + tm - 1) // tm * tm
    padded_end = jnp.cumsum(padded)
    start_padded = padded_end - padded
    dest = start_padded[expert] + rank
    n_rows = 2 * n + N_EXPERTS * tm
    n_blocks = n_rows // tm
    token = jnp.broadcast_to(jnp.arange(n, dtype=jnp.int32)[:, None], (n, 2))
    row_token = jnp.zeros((n_rows,), jnp.int32).at[dest.reshape(-1)].set(token.reshape(-1))
    block_start = jnp.arange(n_blocks, dtype=jnp.int32) * tm
    blk_valid = (block_start < padded_end[-1]).astype(jnp.int32)
    last_expert = jnp.minimum(jnp.searchsorted(padded_end, padded_end[-1] - 1, side="right"), N_EXPERTS - 1)
    blk_expert = jnp.minimum(jnp.searchsorted(padded_end, block_start, side="right"), N_EXPERTS - 1)
    blk_expert = jnp.where(blk_valid > 0, blk_expert, last_expert).astype(jnp.int32)
    yr = _moe_ffn(x32, row_token, w1, w3, w2, blk_expert, blk_valid, tm)
    return _combine(dest.reshape(-1).astype(jnp.int32), yr, gates, x32, ln_g, ln_b)


def _rope_tables(s):
    inv_freq = ROPE_BASE ** (-jnp.arange(0, RET_DK, 2, dtype=F32) / RET_DK)
    ang = jnp.arange(s, dtype=F32)[:, None] * inv_freq[None, :]
    cos = jnp.cos(ang)
    sin = jnp.sin(ang)
    return jnp.concatenate([cos, cos], axis=-1), jnp.concatenate([-sin, sin], axis=-1)


def _even_layer(x32, x16, b, s, w_in, ret_norm_w, conv_w, conv_b, dt_bias, a_log, d_skip,
                ssd_norm_w, w_out, ln1_g, ln1_b, w1, w3, w2, ln2_g, ln2_b):
    n = b * s
    w16 = w_in.astype(BF16)
    w_dt = w16[:, EV_MAIN:].reshape(D_MODEL, SSD_G, SSD_HPG)
    w_dt = jnp.pad(w_dt, ((0, 0), (0, 0), (0, LANES - SSD_HPG))).reshape(D_MODEL, SSD_G * LANES)
    proj = _matmul(x16, w16, 1024, 1024, F32, n=EV_MAIN).reshape(b, s, EV_MAIN)
    dt_raw = _matmul(x16, w_dt, 1024, SSD_G * LANES, F32).reshape(b, s, SSD_G * LANES)
    log_gamma = jnp.log1p(-jnp.exp2(-5.0 - jnp.arange(RET_HEADS, dtype=F32)))
    cos_t, sin_t = _rope_tables(s)
    ret = _retention(proj, log_gamma, cos_t, sin_t, ret_norm_w)
    ssd = _ssd(proj, dt_raw, conv_w, conv_b, _group_lanes(dt_bias), _group_lanes(a_log),
               jnp.repeat(d_skip, SSD_P).reshape(1, SSD_DI), ssd_norm_w)
    x32, x16 = _matmul_ln([ret.reshape(n, RET_V), ssd.reshape(n, SSD_DI)], w_out.astype(BF16),
                          x32, ln1_g, ln1_b)
    return _ffn_ln(x16, w1.astype(BF16), w3.astype(BF16), w2.astype(BF16), x32, ln2_g, ln2_b)


def _odd_layer(x32, x16, b, s, layer, w_in, lb_logits, hg_norm_w, w_out, ln1_g, ln1_b,
               w_router, w1, w3, w2, ln2_g, ln2_b):
    n = b * s
    proj = _matmul(x16, w_in.astype(BF16), 1024, 1024, F32).reshape(b, s, -1)
    o = _hgrn2(proj, lb_logits, hg_norm_w, layer)
    x32, x16 = _matmul_ln([o.reshape(n, -1)], w_out.astype(BF16), x32, ln1_g, ln1_b)
    return _moe(x32, w_router, w1.astype(BF16), w3.astype(BF16), w2.astype(BF16), ln2_g, ln2_b)


def kernel(x, ev_w_in, ev_ret_norm_w, ev_conv_w, ev_conv_b, ev_dt_bias, ev_a_log, ev_d_skip,
           ev_ssd_norm_w, ev_w_out, ev_ln1_g, ev_ln1_b, ffn_w1, ffn_w3, ffn_w2, ev_ln2_g, ev_ln2_b,
           od_w_in, hg_lb_logits, od_hg_norm_w, od_w_out, od_ln1_g, od_ln1_b, moe_router,
           moe_w1, moe_w3, moe_w2, od_ln2_g, od_ln2_b):
    b, s, d = x.shape
    x32 = x.reshape(b * s, d)
    x16 = x32.astype(BF16)
    for layer in range(DEPTH):
        j = layer // 2
        if layer % 2 == 0:
            x32, x16 = _even_layer(x32, x16, b, s, ev_w_in[j], ev_ret_norm_w[j], ev_conv_w[j], ev_conv_b[j],
                                   ev_dt_bias[j], ev_a_log[j], ev_d_skip[j], ev_ssd_norm_w[j], ev_w_out[j],
                                   ev_ln1_g[j], ev_ln1_b[j], ffn_w1[j], ffn_w3[j], ffn_w2[j],
                                   ev_ln2_g[j], ev_ln2_b[j])
        else:
            x32, x16 = _odd_layer(x32, x16, b, s, layer, od_w_in[j], hg_lb_logits, od_hg_norm_w[j],
                                  od_w_out[j], od_ln1_g[j], od_ln1_b[j], moe_router[j],
                                  moe_w1[j], moe_w3[j], moe_w2[j], od_ln2_g[j], od_ln2_b[j])
    return x32.reshape(b, s, d)
```

```python
import functools
import math

import numpy as np
import jax
import jax.numpy as jnp
from jax import lax
from jax.experimental import pallas as pl
from jax.experimental.pallas import tpu as pltpu

F32 = jnp.float32
BF16 = jnp.bfloat16

D_MODEL = 2048
DEPTH = 4
LN_EPS = 1e-5
RET_HEADS = 8
RET_DK = 128
RET_DV = 256
RET_QK = RET_HEADS * RET_DK
RET_V = RET_HEADS * RET_DV
ROPE_BASE = 10000.0
SSD_DI = 2048
SSD_P = 64
SSD_HEADS = 32
SSD_G = 4
SSD_N = 128
SSD_CONV = 4
SSD_XBC = SSD_DI + 2 * SSD_G * SSD_N
SSD_HPG = SSD_HEADS // SSD_G
SSD_GW = SSD_HPG * SSD_P
HG_DK = 128
HG_HEADS = 16
FFN_DIM = 5632
N_EXPERTS = 8
EV_MAIN = 2 * RET_QK + 2 * RET_V + SSD_DI + SSD_XBC
ALPHA = (2.0 * DEPTH) ** 0.25

LANES = 128
SUBLANES = 8
VMEM_LIMIT = 52 * 1024 * 1024
NEG_BIG = -1e30

RET_CHUNK = 256
SSD_CHUNK = 256
HG_CHUNK = 64
HG_ROWS = 512
HG_HPS = 8
MOE_TM = 512
GATHER_ROWS = 256
FFN_SPLIT = 2


def _cparams(*sem):
    return pltpu.CompilerParams(dimension_semantics=sem, vmem_limit_bytes=VMEM_LIMIT)


def _sigmoid(x):
    return 0.5 * jnp.tanh(0.5 * x) + 0.5


def _silu(x):
    return x * _sigmoid(x)


def _bdot(a, b):
    return jnp.dot(a.astype(BF16), b.astype(BF16), preferred_element_type=F32)


def _bdot_nt(a, b):
    return lax.dot_general(a.astype(BF16), b.astype(BF16), (((1,), (1,)), ((), ())),
                           preferred_element_type=F32)


def _bdot_tn(a, b):
    return lax.dot_general(a.astype(BF16), b.astype(BF16), (((0,), (0,)), ((), ())),
                           preferred_element_type=F32)


def _layer_norm_rows(y, g, b):
    mu = jnp.mean(y, axis=-1, keepdims=True)
    yc = y - mu
    var = jnp.mean(yc * yc, axis=-1, keepdims=True)
    return yc * lax.rsqrt(var + LN_EPS) * g + b


def _cumsum_rows(x):
    n = x.shape[0]
    row = lax.broadcasted_iota(jnp.int32, x.shape, 0)
    shift = 1
    while shift < n:
        x = x + jnp.where(row >= shift, pltpu.roll(x, shift, 0), 0.0)
        shift *= 2
    return x


def _mm_kernel(x_ref, w_ref, o_ref):
    o_ref[...] = jnp.dot(x_ref[...], w_ref[...], preferred_element_type=F32).astype(o_ref.dtype)


def _matmul(x, w, layer, tm, tn, out_dtype, n=None):
    m, k = x.shape
    n = w.shape[2] if n is None else n
    return pl.pallas_call(
        _mm_kernel,
        grid=(m // tm, n // tn),
        in_specs=[pl.BlockSpec((tm, k), lambda i, j: (i, 0)),
                  pl.BlockSpec((None, k, tn), lambda i, j: (layer, 0, j))],
        out_specs=pl.BlockSpec((tm, tn), lambda i, j: (i, j)),
        out_shape=jax.ShapeDtypeStruct((m, n), out_dtype),
        compiler_params=_cparams("parallel", "parallel"),
    )(x, w)


def _mm_ln_kernel(*refs, n_in, kpi):
    a_refs = refs[:n_in]
    w_ref, r_ref, g_ref, b_ref, o32_ref, o16_ref, acc_ref = refs[n_in:]
    k = pl.program_id(1)

    @pl.when(k == 0)
    def _():
        acc_ref[...] = jnp.zeros_like(acc_ref)

    def accumulate(a_ref):
        rows = a_ref.shape[0] // FFN_SPLIT
        for r in range(FFN_SPLIT):
            sl = slice(r * rows, (r + 1) * rows)
            acc_ref[sl, :] += jnp.dot(a_ref[sl, :], w_ref[...], preferred_element_type=F32)

    if n_in == 1:
        accumulate(a_refs[0])
    else:
        for j in range(n_in):
            @pl.when((k >= j * kpi) & (k < (j + 1) * kpi))
            def _(j=j):
                accumulate(a_refs[j])

    @pl.when(k == n_in * kpi - 1)
    def _():
        y = ALPHA * r_ref[...] + acc_ref[...]
        out = _layer_norm_rows(y, g_ref[...], b_ref[...])
        o32_ref[...] = out
        o16_ref[...] = out.astype(BF16)


def _matmul_ln(a_list, w, layer, resid, ln_g, ln_b, tm=512, tk=1024):
    n_in = len(a_list)
    m, ka = a_list[0].shape
    kpi = ka // tk
    n = w.shape[2]

    def a_map(j):
        return lambda i, k: (i, jnp.clip(k - j * kpi, 0, kpi - 1))

    in_specs = [pl.BlockSpec((tm, tk), a_map(j)) for j in range(n_in)]
    in_specs += [pl.BlockSpec((None, tk, n), lambda i, k: (layer, k, 0)),
                 pl.BlockSpec((tm, n), lambda i, k: (i, 0)),
                 pl.BlockSpec((1, n), lambda i, k: (0, 0)),
                 pl.BlockSpec((1, n), lambda i, k: (0, 0))]
    return pl.pallas_call(
        functools.partial(_mm_ln_kernel, n_in=n_in, kpi=kpi),
        grid=(m // tm, n_in * kpi),
        in_specs=in_specs,
        out_specs=[pl.BlockSpec((tm, n), lambda i, k: (i, 0)),
                   pl.BlockSpec((tm, n), lambda i, k: (i, 0))],
        out_shape=[jax.ShapeDtypeStruct((m, n), F32), jax.ShapeDtypeStruct((m, n), BF16)],
        scratch_shapes=[pltpu.VMEM((tm, n), F32)],
        compiler_params=_cparams("parallel", "arbitrary"),
    )(*a_list, w, resid, ln_g.reshape(1, n), ln_b.reshape(1, n))


def _swiglu_accumulate(x_ref, w1_ref, w3_ref, w2_ref, o32_ref, j):
    @pl.when(j == 0)
    def _():
        o32_ref[...] = jnp.zeros_like(o32_ref)

    rows = x_ref.shape[0] // FFN_SPLIT
    for r in range(FFN_SPLIT):
        sl = slice(r * rows, (r + 1) * rows)
        x16 = x_ref[sl, :]
        h1 = jnp.dot(x16, w1_ref[...], preferred_element_type=F32)
        h3 = jnp.dot(x16, w3_ref[...], preferred_element_type=F32)
        h = (_silu(h1) * h3).astype(BF16)
        o32_ref[sl, :] += jnp.dot(h, w2_ref[...], preferred_element_type=F32)


def _ffn_ln_kernel(x_ref, w1_ref, w3_ref, w2_ref, r_ref, g_ref, b_ref, o32_ref, o16_ref):
    j = pl.program_id(1)
    _swiglu_accumulate(x_ref, w1_ref, w3_ref, w2_ref, o32_ref, j)

    @pl.when(j == pl.num_programs(1) - 1)
    def _():
        y = ALPHA * r_ref[...] + o32_ref[...]
        out = _layer_norm_rows(y, g_ref[...], b_ref[...])
        o32_ref[...] = out
        o16_ref[...] = out.astype(BF16)


def _ffn_ln(x16, w1, w3, w2, layer, resid, ln_g, ln_b, tm=512, tf=512):
    rows, d = x16.shape
    f = w1.shape[2]
    row_blk = pl.BlockSpec((tm, d), lambda i, j: (i, 0))
    vec = pl.BlockSpec((1, d), lambda i, j: (0, 0))
    return pl.pallas_call(
        _ffn_ln_kernel,
        grid=(rows // tm, f // tf),
        in_specs=[row_blk,
                  pl.BlockSpec((None, d, tf), lambda i, j: (layer, 0, j)),
                  pl.BlockSpec((None, d, tf), lambda i, j: (layer, 0, j)),
                  pl.BlockSpec((None, tf, d), lambda i, j: (layer, j, 0)),
                  row_blk, vec, vec],
        out_specs=[row_blk, row_blk],
        out_shape=[jax.ShapeDtypeStruct((rows, d), F32), jax.ShapeDtypeStruct((rows, d), BF16)],
        compiler_params=_cparams("parallel", "arbitrary"),
    )(x16, w1, w3, w2, resid, ln_g.reshape(1, d), ln_b.reshape(1, d))


def _row_copy(src_hbm, dst_vmem, src_row, dst_row, sem):
    return pltpu.make_async_copy(src_hbm.at[pl.ds(src_row, 1), :], dst_vmem.at[pl.ds(dst_row, 1), :], sem)


def _moe_ffn_kernel(be_ref, bv_ref, tok_ref, x_hbm, w1_ref, w3_ref, w2_ref, o32_ref, buf_ref, x16_ref, sem):
    i = pl.program_id(0)
    j = pl.program_id(1)
    nblk = pl.num_programs(0)
    tm = o32_ref.shape[0]
    valid = bv_ref[i] > 0
    slot = i % 2

    def gather(blk, slt, wait):
        def body(r, carry):
            cp = _row_copy(x_hbm, buf_ref.at[slt], tok_ref[blk * tm + r], r, sem.at[slt])
            if wait:
                cp.wait()
            else:
                cp.start()
            return carry
        lax.fori_loop(0, tm, body, 0, unroll=4)

    @pl.when((j == 0) & (i == 0) & valid)
    def _():
        gather(0, 0, wait=False)

    @pl.when((j == 0) & (i + 1 < nblk))
    def _():
        @pl.when(bv_ref[i + 1] > 0)
        def _():
            gather(i + 1, 1 - slot, wait=False)

    @pl.when((j == 0) & valid)
    def _():
        gather(i, slot, wait=True)
        x16_ref[...] = buf_ref[slot].astype(BF16)

    @pl.when(valid)
    def _():
        _swiglu_accumulate(x16_ref, w1_ref, w3_ref, w2_ref, o32_ref, j)

    @pl.when(jnp.logical_not(valid) & (j == 0))
    def _():
        o32_ref[...] = jnp.zeros_like(o32_ref)


def _moe_ffn(x32, row_token, w1, w3, w2, layer, blk_expert, blk_valid, tm, tf=512):
    n_rows = row_token.shape[0]
    d = x32.shape[1]
    nj = w1.shape[3] // tf

    def jeff(i, j, bv):
        return jnp.where(bv[i] > 0, j, nj - 1)

    return pl.pallas_call(
        _moe_ffn_kernel,
        grid_spec=pltpu.PrefetchScalarGridSpec(
            num_scalar_prefetch=3, grid=(n_rows // tm, nj),
            in_specs=[pl.BlockSpec(memory_space=pl.ANY),
                      pl.BlockSpec((None, None, d, tf), lambda i, j, be, bv, tok: (layer, be[i], 0, jeff(i, j, bv))),
                      pl.BlockSpec((None, None, d, tf), lambda i, j, be, bv, tok: (layer, be[i], 0, jeff(i, j, bv))),
                      pl.BlockSpec((None, None, tf, d), lambda i, j, be, bv, tok: (layer, be[i], jeff(i, j, bv), 0))],
            out_specs=pl.BlockSpec((tm, d), lambda i, j, be, bv, tok: (i, 0)),
            scratch_shapes=[pltpu.VMEM((2, tm, d), F32), pltpu.VMEM((tm, d), BF16),
                            pltpu.SemaphoreType.DMA((2,))]),
        out_shape=jax.ShapeDtypeStruct((n_rows, d), F32),
        compiler_params=_cparams("arbitrary", "arbitrary"),
    )(blk_expert, blk_valid, row_token, x32, w1, w3, w2)


def _retention_kernel(lg_ref, q_ref, k_ref, v_ref, g_ref, cos_ref, sin_ref, nw_ref, o_ref, state_ref):
    c = RET_CHUNK
    h = pl.program_id(1)

    @pl.when(pl.program_id(2) == 0)
    def _():
        state_ref[...] = jnp.zeros_like(state_ref)

    lg = lg_ref[h]
    cos = cos_ref[...]
    sin = sin_ref[...]
    q = q_ref[...]
    k = k_ref[...]
    qr = q * cos + pltpu.roll(q, RET_DK // 2, 1) * sin
    kr = (k * cos + pltpu.roll(k, RET_DK // 2, 1) * sin) * (RET_DK ** -0.5)
    t = lax.broadcasted_iota(jnp.int32, (c, 1), 0).astype(F32)
    tt = lax.broadcasted_iota(jnp.int32, (c, c), 0)
    ss = lax.broadcasted_iota(jnp.int32, (c, c), 1)
    decay = jnp.exp(jnp.where(tt >= ss, (tt - ss).astype(F32) * lg, NEG_BIG))
    v = v_ref[...].astype(BF16)
    state = state_ref[...]
    scores = _bdot_nt(qr, kr) * decay
    y = jnp.dot(scores.astype(BF16), v, preferred_element_type=F32)
    y = y + _bdot(qr * jnp.exp((t + 1.0) * lg), state)
    k_dec = kr * jnp.exp((float(c - 1) - t) * lg)
    state_ref[...] = jnp.exp(float(c) * lg) * state + _bdot_tn(k_dec, v)
    mu = jnp.mean(y, axis=-1, keepdims=True)
    yc = y - mu
    var = jnp.mean(yc * yc, axis=-1, keepdims=True)
    yn = yc * lax.rsqrt(var + LN_EPS) * nw_ref[...]
    o_ref[...] = (yn * _silu(g_ref[...])).astype(o_ref.dtype)


def _retention(proj, log_gamma, cos_t, sin_t, norm_w):
    b, s, _ = proj.shape
    c = RET_CHUNK
    kb = RET_QK // RET_DK
    vb = 2 * RET_QK // RET_DV
    gb = vb + RET_V // RET_DV
    return pl.pallas_call(
        _retention_kernel,
        grid_spec=pltpu.PrefetchScalarGridSpec(
            num_scalar_prefetch=1, grid=(b, RET_HEADS, s // c),
            in_specs=[pl.BlockSpec((None, c, RET_DK), lambda bi, h, ci, lg: (bi, ci, h)),
                      pl.BlockSpec((None, c, RET_DK), lambda bi, h, ci, lg: (bi, ci, kb + h)),
                      pl.BlockSpec((None, c, RET_DV), lambda bi, h, ci, lg: (bi, ci, vb + h)),
                      pl.BlockSpec((None, c, RET_DV), lambda bi, h, ci, lg: (bi, ci, gb + h)),
                      pl.BlockSpec((c, RET_DK), lambda bi, h, ci, lg: (ci, 0)),
                      pl.BlockSpec((c, RET_DK), lambda bi, h, ci, lg: (ci, 0)),
                      pl.BlockSpec((1, RET_DV), lambda bi, h, ci, lg: (0, h))],
            out_specs=pl.BlockSpec((None, c, RET_DV), lambda bi, h, ci, lg: (bi, ci, h)),
            scratch_shapes=[pltpu.VMEM((RET_DK, RET_DV), F32)]),
        out_shape=jax.ShapeDtypeStruct((b, s, RET_V), BF16),
        compiler_params=_cparams("parallel", "parallel", "arbitrary"),
    )(log_gamma, proj, proj, proj, proj, cos_t, sin_t, norm_w.reshape(1, RET_V))


def _softplus(x):
    return jnp.maximum(x, 0.0) + jnp.log1p(jnp.exp(-jnp.abs(x)))


def _ssd_kernel(z_ref, xs_ref, bm_ref, cm_ref, dt_ref, cwx_ref, cwb_ref, cwc_ref,
                cbx_ref, cbb_ref, cbc_ref, dtb_ref, alog_ref, dsk_ref, nw_ref,
                o_ref, state_ref, ext_ref):
    c = SSD_CHUNK
    gw = SSD_GW
    n = SSD_N
    pad = SUBLANES

    @pl.when(pl.program_id(2) == 0)
    def _():
        state_ref[...] = jnp.zeros_like(state_ref)
        ext_ref[0:pad, :] = jnp.zeros((pad, gw + 2 * n), F32)

    ext_ref[pad:pad + c, 0:gw] = xs_ref[...]
    ext_ref[pad:pad + c, gw:gw + n] = bm_ref[...]
    ext_ref[pad:pad + c, gw + n:gw + 2 * n] = cm_ref[...]
    cw = jnp.concatenate([cwx_ref[...], cwb_ref[...], cwc_ref[...]], axis=1)
    cb = jnp.concatenate([cbx_ref[...], cbb_ref[...], cbc_ref[...]], axis=1)
    conv = cb + cw[SSD_CONV - 1:SSD_CONV, :] * ext_ref[pad:pad + c, :]
    for j in range(SSD_CONV - 1):
        off = pad - (SSD_CONV - 1) + j
        conv = conv + cw[j:j + 1, :] * ext_ref[off:off + c, :]
    tail = ext_ref[c:c + pad, :]
    ext_ref[0:pad, :] = tail
    conv = _silu(conv)
    xs = conv[:, 0:gw]
    bm = conv[:, gw:gw + n]
    cm = conv[:, gw + n:gw + 2 * n]

    dt = _softplus(dt_ref[...] + dtb_ref[...])
    log_a = -dt * jnp.exp(alog_ref[...])
    cum = _cumsum_rows(log_a)
    cum_t = cum.T
    last = cum[c - 1:c, :]
    e_cum = jnp.exp(cum)
    e_rem = jnp.exp(last - cum)
    e_last = jnp.exp(last)

    tt = lax.broadcasted_iota(jnp.int32, (c, c), 0)
    ss = lax.broadcasted_iota(jnp.int32, (c, c), 1)
    causal = tt >= ss
    lane = lax.broadcasted_iota(jnp.int32, (c, 2 * SSD_P), 1)
    first = lane < SSD_P
    lane1 = lax.broadcasted_iota(jnp.int32, (1, 2 * SSD_P), 1)
    first1 = lane1 < SSD_P

    state = state_ref[...]
    gmat = _bdot_nt(cm, bm)
    cross = _bdot(cm, state)
    ys, xws, els = [], [], []
    for p in range(SSD_HPG // 2):
        ha, hb = 2 * p, 2 * p + 1
        lo, hi = p * 2 * SSD_P, (p + 1) * 2 * SSD_P
        xs_p = xs[:, lo:hi]
        xdt = xs_p * jnp.where(first, dt[:, ha:ha + 1], dt[:, hb:hb + 1])
        xdt16 = xdt.astype(BF16)
        dec_a = jnp.exp(jnp.where(causal, cum[:, ha:ha + 1] - cum_t[ha:ha + 1, :], NEG_BIG))
        dec_b = jnp.exp(jnp.where(causal, cum[:, hb:hb + 1] - cum_t[hb:hb + 1, :], NEG_BIG))
        ya = jnp.dot((gmat * dec_a).astype(BF16), xdt16, preferred_element_type=F32)
        yb = jnp.dot((gmat * dec_b).astype(BF16), xdt16, preferred_element_type=F32)
        y = jnp.where(first, ya, yb)
        y = y + cross[:, lo:hi] * jnp.where(first, e_cum[:, ha:ha + 1], e_cum[:, hb:hb + 1])
        y = y + dsk_ref[:, lo:hi] * xs_p
        ys.append(y)
        xws.append(xdt * jnp.where(first, e_rem[:, ha:ha + 1], e_rem[:, hb:hb + 1]))
        els.append(jnp.where(first1, e_last[:, ha:ha + 1], e_last[:, hb:hb + 1]))
    y = jnp.concatenate(ys, axis=1)
    xw = jnp.concatenate(xws, axis=1)
    el = jnp.concatenate(els, axis=1)
    state_ref[...] = state * el + _bdot_tn(bm, xw)

    yz = y * _silu(z_ref[...])
    ms = jnp.mean(yz * yz, axis=-1, keepdims=True)
    o_ref[...] = (yz * lax.rsqrt(ms + LN_EPS) * nw_ref[...]).astype(o_ref.dtype)


def _ssd(proj, dt_raw, conv_w, conv_b, dtb, alog, dskip, norm_w):
    b, s, _ = proj.shape
    c = SSD_CHUNK
    gw = SSD_GW
    n = SSD_N
    z0 = (2 * RET_QK + 2 * RET_V) // gw
    x0 = z0 + SSD_DI // gw
    b0 = (2 * RET_QK + 2 * RET_V + SSD_DI + SSD_DI) // n
    c0 = b0 + SSD_G
    wb0 = SSD_DI // n
    wc0 = wb0 + SSD_G
    im3 = lambda f: (lambda bi, g, ci: f(bi, g, ci))
    return pl.pallas_call(
        _ssd_kernel,
        grid=(b, SSD_G, s // c),
        in_specs=[pl.BlockSpec((None, c, gw), im3(lambda bi, g, ci: (bi, ci, z0 + g))),
                  pl.BlockSpec((None, c, gw), im3(lambda bi, g, ci: (bi, ci, x0 + g))),
                  pl.BlockSpec((None, c, n), im3(lambda bi, g, ci: (bi, ci, b0 + g))),
                  pl.BlockSpec((None, c, n), im3(lambda bi, g, ci: (bi, ci, c0 + g))),
                  pl.BlockSpec((None, c, LANES), im3(lambda bi, g, ci: (bi, ci, g))),
                  pl.BlockSpec((SSD_CONV, gw), im3(lambda bi, g, ci: (0, g))),
                  pl.BlockSpec((SSD_CONV, n), im3(lambda bi, g, ci: (0, wb0 + g))),
                  pl.BlockSpec((SSD_CONV, n), im3(lambda bi, g, ci: (0, wc0 + g))),
                  pl.BlockSpec((1, gw), im3(lambda bi, g, ci: (0, g))),
                  pl.BlockSpec((1, n), im3(lambda bi, g, ci: (0, wb0 + g))),
                  pl.BlockSpec((1, n), im3(lambda bi, g, ci: (0, wc0 + g))),
                  pl.BlockSpec((1, LANES), im3(lambda bi, g, ci: (0, g))),
                  pl.BlockSpec((1, LANES), im3(lambda bi, g, ci: (0, g))),
                  pl.BlockSpec((1, gw), im3(lambda bi, g, ci: (0, g))),
                  pl.BlockSpec((1, gw), im3(lambda bi, g, ci: (0, g)))],
        out_specs=pl.BlockSpec((None, c, gw), im3(lambda bi, g, ci: (bi, ci, g))),
        out_shape=jax.ShapeDtypeStruct((b, s, SSD_DI), BF16),
        scratch_shapes=[pltpu.VMEM((n, gw), F32), pltpu.VMEM((c + SUBLANES, gw + 2 * n), F32)],
        compiler_params=_cparams("parallel", "parallel", "arbitrary"),
    )(proj, proj, proj, proj, dt_raw, conv_w, conv_w, conv_w,
      conv_b.reshape(1, SSD_XBC), conv_b.reshape(1, SSD_XBC), conv_b.reshape(1, SSD_XBC),
      dtb, alog, dskip, norm_w.reshape(1, SSD_DI))


def _group_lanes(v):
    v = v.reshape(SSD_G, SSD_HPG)
    return jnp.pad(v, ((0, 0), (0, LANES - SSD_HPG))).reshape(1, SSD_G * LANES)


def _hgrn2_masks():
    c = HG_CHUNK
    t = lax.broadcasted_iota(jnp.int32, (c, c), 0)
    s = lax.broadcasted_iota(jnp.int32, (c, c), 1)
    masks = []
    blk = c
    while blk > SUBLANES:
        half = blk // 2
        m = (t // blk == s // blk) & ((t % blk) >= half) & ((s % blk) < half)
        masks.append((blk, m))
        blk = half
    return masks


def _hgrn2_kernel(q_ref, f_ref, i_ref, g_ref, lbl_ref, nw_ref, o_ref, state_ref, *, layer):
    c = HG_CHUNK
    dk = HG_DK

    @pl.when(pl.program_id(2) == 0)
    def _():
        state_ref[...] = jnp.zeros_like(state_ref)

    lbl = lbl_ref[...]
    ex = jnp.exp(lbl - jnp.max(lbl, axis=0, keepdims=True))
    pr = ex / jnp.sum(ex, axis=0, keepdims=True)
    lb_all = jnp.sum(pr[1:layer + 1, :], axis=0, keepdims=True)
    nw_all = nw_ref[...]

    masks = _hgrn2_masks()
    row = lax.broadcasted_iota(jnp.int32, (c, dk), 0)
    sub = row % SUBLANES
    trow = lax.broadcasted_iota(jnp.int32, (c, c), 0)
    scol = lax.broadcasted_iota(jnp.int32, (c, c), 1)
    tile_base = (trow // SUBLANES) * SUBLANES

    def sub_bcast(x, j):
        x3 = x.reshape(c // SUBLANES, SUBLANES, dk)
        return jnp.broadcast_to(x3[:, j:j + 1, :], x3.shape).reshape(c, dk)

    def head_chunk(r0, hh):
        cols = slice(hh * dk, (hh + 1) * dk)
        lb = lb_all[:, cols]
        fr = f_ref[pl.ds(r0, c), cols]
        sg = _sigmoid(fr)
        log2_f = jnp.log2(lb + (1.0 - lb) * sg)
        kk = (1.0 - lb) * (1.0 - sg)
        qq = _silu(q_ref[pl.ds(r0, c), cols])
        v16 = i_ref[pl.ds(r0, c), cols].astype(BF16)
        cum = _cumsum_rows(log2_f)

        scores = jnp.zeros((c, c), F32)
        for blk, m in masks:
            half = blk // 2
            refv = jnp.concatenate(
                [jnp.broadcast_to(cum[nb * blk + half - 1:nb * blk + half, :], (blk, dk))
                 for nb in range(c // blk)], axis=0)
            e = jnp.exp2(-jnp.abs(cum - refv))
            scores = scores + jnp.where(m, _bdot_nt(qq * e, kk * e), 0.0)
        for j in range(SUBLANES):
            e = jnp.exp2(jnp.where(sub >= j, cum - sub_bcast(cum, j), NEG_BIG))
            col = jnp.sum(qq * e * sub_bcast(kk, j), axis=-1, keepdims=True)
            scores = scores + jnp.where(scol == tile_base + j, col, 0.0)

        state_t = state_ref[hh]
        y = jnp.dot(scores.astype(BF16), v16, preferred_element_type=F32)
        y = y + _bdot_nt(qq * jnp.exp2(cum), state_t)
        last = cum[c - 1:c, :]
        k_dec = kk * jnp.exp2(last - cum)
        state_ref[hh] = state_t * jnp.exp2(last) + _bdot_tn(v16, k_dec)
        ms = jnp.mean(y * y, axis=-1, keepdims=True)
        out = y * lax.rsqrt(ms + LN_EPS) * nw_all[:, cols] * _silu(g_ref[pl.ds(r0, c), cols])
        o_ref[pl.ds(r0, c), cols] = out.astype(o_ref.dtype)

    def chunk(ci, carry):
        r0 = pl.multiple_of(ci * c, c)
        for hh in range(HG_HPS):
            head_chunk(r0, hh)
        return carry

    lax.fori_loop(0, HG_ROWS // c, chunk, 0)


def _hgrn2(proj, lb_logits, norm_w, layer):
    b, s, _ = proj.shape
    r = HG_ROWS
    w = HG_HPS * HG_DK
    nb = HG_HEADS // HG_HPS
    return pl.pallas_call(
        functools.partial(_hgrn2_kernel, layer=layer),
        grid=(b, nb, s // r),
        in_specs=[pl.BlockSpec((None, r, w), lambda bi, hi, ri: (bi, ri, hi)),
                  pl.BlockSpec((None, r, w), lambda bi, hi, ri: (bi, ri, nb + hi)),
                  pl.BlockSpec((None, r, w), lambda bi, hi, ri: (bi, ri, 2 * nb + hi)),
                  pl.BlockSpec((None, r, w), lambda bi, hi, ri: (bi, ri, 3 * nb + hi)),
                  pl.BlockSpec((DEPTH, w), lambda bi, hi, ri: (0, hi)),
                  pl.BlockSpec((1, w), lambda bi, hi, ri: (0, hi))],
        out_specs=pl.BlockSpec((None, r, w), lambda bi, hi, ri: (bi, ri, hi)),
        out_shape=jax.ShapeDtypeStruct((b, s, HG_HEADS * HG_DK), BF16),
        scratch_shapes=[pltpu.VMEM((HG_HPS, HG_DK, HG_DK), F32)],
        compiler_params=_cparams("parallel", "parallel", "arbitrary"),
    )(proj, proj, proj, proj, lb_logits, norm_w.reshape(1, HG_HEADS * HG_DK))


def _router_kernel(x_ref, w_ref, idx_ref, gate_ref, cnt_ref, carry_ref):
    tm = x_ref.shape[0]

    @pl.when(pl.program_id(0) == 0)
    def _():
        carry_ref[...] = jnp.zeros_like(carry_ref)

    logits = jnp.dot(x_ref[...], w_ref[...], preferred_element_type=F32,
                     precision=lax.Precision.HIGHEST)
    lane = lax.broadcasted_iota(jnp.int32, (tm, LANES), 1)
    lane_f = lane.astype(F32)
    lg = jnp.where(lane < N_EXPERTS, logits, NEG_BIG)
    m1 = jnp.max(lg, axis=-1, keepdims=True)
    i1 = jnp.min(jnp.where(lg == m1, lane_f, float(LANES)), axis=-1, keepdims=True).astype(jnp.int32)
    lg2 = jnp.where(lane == i1, NEG_BIG, lg)
    m2 = jnp.max(lg2, axis=-1, keepdims=True)
    i2 = jnp.min(jnp.where(lg2 == m2, lane_f, float(LANES)), axis=-1, keepdims=True).astype(jnp.int32)
    e2 = jnp.exp(m2 - m1)
    den = 1.0 + e2
    gate_ref[...] = jnp.where(lane == 0, 1.0 / den, jnp.where(lane == 1, e2 / den, 0.0))

    sel1 = lane == i1
    sel2 = lane == i2
    assign = (sel1 | sel2).astype(BF16)
    tt = lax.broadcasted_iota(jnp.int32, (tm, tm), 0)
    ss = lax.broadcasted_iota(jnp.int32, (tm, tm), 1)
    before = (ss < tt).astype(BF16)
    base = carry_ref[...] + jnp.dot(before, assign, preferred_element_type=F32)
    r1 = jnp.sum(jnp.where(sel1, base, 0.0), axis=-1, keepdims=True).astype(jnp.int32)
    r2 = jnp.sum(jnp.where(sel2, base, 0.0), axis=-1, keepdims=True).astype(jnp.int32)
    idx_ref[...] = jnp.where(lane == 0, i1, jnp.where(lane == 1, i2,
                             jnp.where(lane == 2, r1, jnp.where(lane == 3, r2, 0))))
    carry_ref[...] += jnp.sum(assign.astype(F32), axis=0, keepdims=True)
    cnt_ref[...] = carry_ref[...]


def _router(x32, w_router, tm=512):
    n, d = x32.shape
    wr = jnp.pad(w_router, ((0, 0), (0, LANES - N_EXPERTS)))
    return pl.pallas_call(
        _router_kernel,
        grid=(n // tm,),
        in_specs=[pl.BlockSpec((tm, d), lambda i: (i, 0)),
                  pl.BlockSpec((d, LANES), lambda i: (0, 0))],
        out_specs=[pl.BlockSpec((tm, LANES), lambda i: (i, 0)),
                   pl.BlockSpec((tm, LANES), lambda i: (i, 0)),
                   pl.BlockSpec((1, LANES), lambda i: (0, 0))],
        out_shape=[jax.ShapeDtypeStruct((n, LANES), jnp.int32),
                   jax.ShapeDtypeStruct((n, LANES), F32),
                   jax.ShapeDtypeStruct((1, LANES), F32)],
        scratch_shapes=[pltpu.VMEM((1, LANES), F32)],
        compiler_params=_cparams("arbitrary"),
    )(x32, wr)


def _combine_kernel(pos_ref, y_hbm, gate_ref, x_ref, g_ref, b_ref, o32_ref, o16_ref, buf_ref, sem):
    i = pl.program_id(0)
    nblk = pl.num_programs(0)
    tc = x_ref.shape[0]
    slot = i % 2

    def gather(blk, slt, wait):
        def body(r, carry):
            for k in range(2):
                cp = _row_copy(y_hbm, buf_ref.at[slt, k], pos_ref[2 * (blk * tc + r) + k], r, sem.at[slt])
                if wait:
                    cp.wait()
                else:
                    cp.start()
            return carry
        lax.fori_loop(0, tc, body, 0, unroll=2)

    @pl.when(i == 0)
    def _():
        gather(0, 0, wait=False)

    @pl.when(i + 1 < nblk)
    def _():
        gather(i + 1, 1 - slot, wait=False)

    gather(i, slot, wait=True)
    gates = gate_ref[...]
    ffn = gates[:, 0:1] * buf_ref[slot, 0] + gates[:, 1:2] * buf_ref[slot, 1]
    out = _layer_norm_rows(ALPHA * x_ref[...] + ffn, g_ref[...], b_ref[...])
    o32_ref[...] = out
    o16_ref[...] = out.astype(BF16)


def _combine(pos, yr, gates, x32, ln_g, ln_b, tc=GATHER_ROWS):
    n, d = x32.shape
    return pl.pallas_call(
        _combine_kernel,
        grid_spec=pltpu.PrefetchScalarGridSpec(
            num_scalar_prefetch=1, grid=(n // tc,),
            in_specs=[pl.BlockSpec(memory_space=pl.ANY),
                      pl.BlockSpec((tc, LANES), lambda i, p: (i, 0)),
                      pl.BlockSpec((tc, d), lambda i, p: (i, 0)),
                      pl.BlockSpec((1, d), lambda i, p: (0, 0)),
                      pl.BlockSpec((1, d), lambda i, p: (0, 0))],
            out_specs=[pl.BlockSpec((tc, d), lambda i, p: (i, 0)),
                       pl.BlockSpec((tc, d), lambda i, p: (i, 0))],
            scratch_shapes=[pltpu.VMEM((2, 2, tc, d), F32), pltpu.SemaphoreType.DMA((2,))]),
        out_shape=[jax.ShapeDtypeStruct((n, d), F32), jax.ShapeDtypeStruct((n, d), BF16)],
        compiler_params=_cparams("arbitrary"),
    )(pos, yr, gates, x32, ln_g.reshape(1, d), ln_b.reshape(1, d))


def _moe(x32, w_router, w1, w3, w2, layer, ln_g, ln_b):
    n, d = x32.shape
    tm = MOE_TM
    idx, gates, counts = _router(x32, w_router)
    counts = counts[0, :N_EXPERTS].astype(jnp.int32)
    expert = idx[:, 0:2]
    rank = idx[:, 2:4]
    padded = (counts + tm - 1) // tm * tm
    padded_end = jnp.cumsum(padded)
    start_padded = padded_end - padded
    dest = start_padded[expert] + rank
    n_rows = 2 * n + N_EXPERTS * tm
    n_blocks = n_rows // tm
    token = jnp.broadcast_to(jnp.arange(n, dtype=jnp.int32)[:, None], (n, 2))
    row_token = jnp.zeros((n_rows,), jnp.int32).at[dest.reshape(-1)].set(token.reshape(-1))
    block_start = jnp.arange(n_blocks, dtype=jnp.int32) * tm
    blk_valid = (block_start < padded_end[-1]).astype(jnp.int32)
    last_expert = jnp.minimum(jnp.searchsorted(padded_end, padded_end[-1] - 1, side="right"), N_EXPERTS - 1)
    blk_expert = jnp.minimum(jnp.searchsorted(padded_end, block_start, side="right"), N_EXPERTS - 1)
    blk_expert = jnp.where(blk_valid > 0, blk_expert, last_expert).astype(jnp.int32)
    yr = _moe_ffn(x32, row_token, w1, w3, w2, layer, blk_expert, blk_valid, tm)
    return _combine(dest.reshape(-1).astype(jnp.int32), yr, gates, x32, ln_g, ln_b)


def _rope_tables(s):
    inv_freq = ROPE_BASE ** (-jnp.arange(0, RET_DK, 2, dtype=F32) / RET_DK)
    ang = jnp.arange(s, dtype=F32)[:, None] * inv_freq[None, :]
    cos = jnp.cos(ang)
    sin = jnp.sin(ang)
    return jnp.concatenate([cos, cos], axis=-1), jnp.concatenate([-sin, sin], axis=-1)


def _even_layer(x32, x16, b, s, j, w_in, ret_norm_w, conv_w, conv_b, dt_bias, a_log, d_skip,
                ssd_norm_w, w_out, ln1_g, ln1_b, w1, w3, w2, ln2_g, ln2_b):
    n = b * s
    w_dt = w_in[j, :, EV_MAIN:].reshape(D_MODEL, SSD_G, SSD_HPG)
    w_dt = jnp.pad(w_dt, ((0, 0), (0, 0), (0, LANES - SSD_HPG))).reshape(1, D_MODEL, SSD_G * LANES)
    proj = _matmul(x16, w_in, j, 1024, 1024, F32, n=EV_MAIN).reshape(b, s, EV_MAIN)
    dt_raw = _matmul(x16, w_dt, 0, 1024, SSD_G * LANES, F32).reshape(b, s, SSD_G * LANES)
    log_gamma = jnp.log1p(-jnp.exp2(-5.0 - jnp.arange(RET_HEADS, dtype=F32)))
    cos_t, sin_t = _rope_tables(s)
    ret = _retention(proj, log_gamma, cos_t, sin_t, ret_norm_w)
    ssd = _ssd(proj, dt_raw, conv_w, conv_b, _group_lanes(dt_bias), _group_lanes(a_log),
               jnp.repeat(d_skip, SSD_P).reshape(1, SSD_DI), ssd_norm_w)
    x32, x16 = _matmul_ln([ret.reshape(n, RET_V), ssd.reshape(n, SSD_DI)], w_out, j, x32, ln1_g, ln1_b)
    return _ffn_ln(x16, w1, w3, w2, j, x32, ln2_g, ln2_b)


def _odd_layer(x32, x16, b, s, layer, w_in, lb_logits, hg_norm_w, w_out, ln1_g, ln1_b,
               w_router, w1, w3, w2, ln2_g, ln2_b):
    n = b * s
    j = layer // 2
    proj = _matmul(x16, w_in, j, 1024, 1024, F32).reshape(b, s, -1)
    o = _hgrn2(proj, lb_logits, hg_norm_w, layer)
    x32, x16 = _matmul_ln([o.reshape(n, -1)], w_out, j, x32, ln1_g, ln1_b)
    return _moe(x32, w_router, w1, w3, w2, j, ln2_g, ln2_b)


def kernel(x, ev_w_in, ev_ret_norm_w, ev_conv_w, ev_conv_b, ev_dt_bias, ev_a_log, ev_d_skip,
           ev_ssd_norm_w, ev_w_out, ev_ln1_g, ev_ln1_b, ffn_w1, ffn_w3, ffn_w2, ev_ln2_g, ev_ln2_b,
           od_w_in, hg_lb_logits, od_hg_norm_w, od_w_out, od_ln1_g, od_ln1_b, moe_router,
           moe_w1, moe_w3, moe_w2, od_ln2_g, od_ln2_b):
    b, s, d = x.shape
    x32 = x.reshape(b * s, d)
    x16 = x32.astype(BF16)
    ev_w_in, ev_w_out, ffn_w1, ffn_w3, ffn_w2, od_w_in, od_w_out, moe_w1, moe_w3, moe_w2 = (
        w.astype(BF16) for w in (ev_w_in, ev_w_out, ffn_w1, ffn_w3, ffn_w2, od_w_in, od_w_out,
                                 moe_w1, moe_w3, moe_w2))
    for layer in range(DEPTH):
        j = layer // 2
        if layer % 2 == 0:
            x32, x16 = _even_layer(x32, x16, b, s, j, ev_w_in, ev_ret_norm_w[j], ev_conv_w[j], ev_conv_b[j],
                                   ev_dt_bias[j], ev_a_log[j], ev_d_skip[j], ev_ssd_norm_w[j], ev_w_out,
                                   ev_ln1_g[j], ev_ln1_b[j], ffn_w1, ffn_w3, ffn_w2,
                                   ev_ln2_g[j], ev_ln2_b[j])
        else:
            x32, x16 = _odd_layer(x32, x16, b, s, layer, od_w_in, hg_lb_logits, od_hg_norm_w[j],
                                  od_w_out, od_ln1_g[j], od_ln1_b[j], moe_router[j],
                                  moe_w1, moe_w3, moe_w2, od_ln2_g[j], od_ln2_b[j])
    return x32.reshape(b, s, d)
```

```python
import functools
import math

import numpy as np
import jax
import jax.numpy as jnp
from jax import lax
from jax.experimental import pallas as pl
from jax.experimental.pallas import tpu as pltpu

F32 = jnp.float32
BF16 = jnp.bfloat16

D_MODEL = 2048
DEPTH = 4
LN_EPS = 1e-5
RET_HEADS = 8
RET_DK = 128
RET_DV = 256
RET_QK = RET_HEADS * RET_DK
RET_V = RET_HEADS * RET_DV
ROPE_BASE = 10000.0
SSD_DI = 2048
SSD_P = 64
SSD_HEADS = 32
SSD_G = 4
SSD_N = 128
SSD_CONV = 4
SSD_XBC = SSD_DI + 2 * SSD_G * SSD_N
SSD_HPG = SSD_HEADS // SSD_G
SSD_GW = SSD_HPG * SSD_P
HG_DK = 128
HG_HEADS = 16
FFN_DIM = 5632
N_EXPERTS = 8
EV_MAIN = 2 * RET_QK + 2 * RET_V + SSD_DI + SSD_XBC
ALPHA = (2.0 * DEPTH) ** 0.25

LANES = 128
SUBLANES = 8
VMEM_LIMIT = 52 * 1024 * 1024
NEG_BIG = -1e30
LOG2_E = math.log2(math.e)

RET_CHUNK = 256
RET_HPS = 4
SSD_CHUNK = 256
HG_CHUNK = 64
HG_ROWS = 512
HG_HPS = 8
MOE_TM = 512
GATHER_ROWS = 256
FFN_SPLIT = 2


def _cparams(*sem):
    return pltpu.CompilerParams(dimension_semantics=sem, vmem_limit_bytes=VMEM_LIMIT)


def _sigmoid(x):
    return 0.5 * jnp.tanh(0.5 * x) + 0.5


def _silu(x):
    return x * _sigmoid(x)


def _bdot(a, b):
    return jnp.dot(a.astype(BF16), b.astype(BF16), preferred_element_type=F32)


def _bdot_nt(a, b):
    return lax.dot_general(a.astype(BF16), b.astype(BF16), (((1,), (1,)), ((), ())),
                           preferred_element_type=F32)


def _bdot_tn(a, b):
    return lax.dot_general(a.astype(BF16), b.astype(BF16), (((0,), (0,)), ((), ())),
                           preferred_element_type=F32)


def _layer_norm_rows(y, g, b):
    mu = jnp.mean(y, axis=-1, keepdims=True)
    yc = y - mu
    var = jnp.mean(yc * yc, axis=-1, keepdims=True)
    return yc * lax.rsqrt(var + LN_EPS) * g + b


def _cumsum_rows(x):
    n = x.shape[0]
    row = lax.broadcasted_iota(jnp.int32, x.shape, 0)
    shift = 1
    while shift < n:
        x = x + jnp.where(row >= shift, pltpu.roll(x, shift, 0), 0.0)
        shift *= 2
    return x


def _mm_kernel(x_ref, w_ref, o_ref):
    o_ref[...] = jnp.dot(x_ref[...], w_ref[...], preferred_element_type=F32).astype(o_ref.dtype)


def _matmul(x, w, layer, tm, tn, out_dtype, n=None):
    m, k = x.shape
    n = w.shape[2] if n is None else n
    return pl.pallas_call(
        _mm_kernel,
        grid=(m // tm, n // tn),
        in_specs=[pl.BlockSpec((tm, k), lambda i, j: (i, 0)),
                  pl.BlockSpec((None, k, tn), lambda i, j: (layer, 0, j))],
        out_specs=pl.BlockSpec((tm, tn), lambda i, j: (i, j)),
        out_shape=jax.ShapeDtypeStruct((m, n), out_dtype),
        compiler_params=_cparams("parallel", "parallel"),
    )(x, w)


def _mm_ln_kernel(*refs, n_in, kpi):
    a_refs = refs[:n_in]
    w_ref, r_ref, g_ref, b_ref, o32_ref, o16_ref, acc_ref = refs[n_in:]
    k = pl.program_id(1)

    @pl.when(k == 0)
    def _():
        acc_ref[...] = jnp.zeros_like(acc_ref)

    def accumulate(a_ref):
        rows = a_ref.shape[0] // FFN_SPLIT
        for r in range(FFN_SPLIT):
            sl = slice(r * rows, (r + 1) * rows)
            acc_ref[sl, :] += jnp.dot(a_ref[sl, :], w_ref[...], preferred_element_type=F32)

    if n_in == 1:
        accumulate(a_refs[0])
    else:
        for j in range(n_in):
            @pl.when((k >= j * kpi) & (k < (j + 1) * kpi))
            def _(j=j):
                accumulate(a_refs[j])

    @pl.when(k == n_in * kpi - 1)
    def _():
        y = ALPHA * r_ref[...] + acc_ref[...]
        out = _layer_norm_rows(y, g_ref[...], b_ref[...])
        o32_ref[...] = out
        o16_ref[...] = out.astype(BF16)


def _matmul_ln(a_list, w, layer, resid, ln_g, ln_b, tm=512, tk=1024):
    n_in = len(a_list)
    m, ka = a_list[0].shape
    kpi = ka // tk
    n = w.shape[2]

    def a_map(j):
        return lambda i, k: (i, jnp.clip(k - j * kpi, 0, kpi - 1))

    in_specs = [pl.BlockSpec((tm, tk), a_map(j)) for j in range(n_in)]
    in_specs += [pl.BlockSpec((None, tk, n), lambda i, k: (layer, k, 0)),
                 pl.BlockSpec((tm, n), lambda i, k: (i, 0)),
                 pl.BlockSpec((1, n), lambda i, k: (0, 0)),
                 pl.BlockSpec((1, n), lambda i, k: (0, 0))]
    return pl.pallas_call(
        functools.partial(_mm_ln_kernel, n_in=n_in, kpi=kpi),
        grid=(m // tm, n_in * kpi),
        in_specs=in_specs,
        out_specs=[pl.BlockSpec((tm, n), lambda i, k: (i, 0)),
                   pl.BlockSpec((tm, n), lambda i, k: (i, 0))],
        out_shape=[jax.ShapeDtypeStruct((m, n), F32), jax.ShapeDtypeStruct((m, n), BF16)],
        scratch_shapes=[pltpu.VMEM((tm, n), F32)],
        compiler_params=_cparams("parallel", "arbitrary"),
    )(*a_list, w, resid, ln_g.reshape(1, n), ln_b.reshape(1, n))


def _swiglu_accumulate(x_ref, w1_ref, w3_ref, w2_ref, o32_ref, j):
    @pl.when(j == 0)
    def _():
        o32_ref[...] = jnp.zeros_like(o32_ref)

    rows = x_ref.shape[0] // FFN_SPLIT
    for r in range(FFN_SPLIT):
        sl = slice(r * rows, (r + 1) * rows)
        x16 = x_ref[sl, :]
        h1 = jnp.dot(x16, w1_ref[...], preferred_element_type=F32)
        h3 = jnp.dot(x16, w3_ref[...], preferred_element_type=F32)
        h = (_silu(h1) * h3).astype(BF16)
        o32_ref[sl, :] += jnp.dot(h, w2_ref[...], preferred_element_type=F32)


def _ffn_ln_kernel(x_ref, w1_ref, w3_ref, w2_ref, r_ref, g_ref, b_ref, o32_ref, o16_ref):
    j = pl.program_id(1)
    _swiglu_accumulate(x_ref, w1_ref, w3_ref, w2_ref, o32_ref, j)

    @pl.when(j == pl.num_programs(1) - 1)
    def _():
        y = ALPHA * r_ref[...] + o32_ref[...]
        out = _layer_norm_rows(y, g_ref[...], b_ref[...])
        o32_ref[...] = out
        o16_ref[...] = out.astype(BF16)


def _ffn_ln(x16, w1, w3, w2, layer, resid, ln_g, ln_b, tm=512, tf=512):
    rows, d = x16.shape
    f = w1.shape[2]
    row_blk = pl.BlockSpec((tm, d), lambda i, j: (i, 0))
    vec = pl.BlockSpec((1, d), lambda i, j: (0, 0))
    return pl.pallas_call(
        _ffn_ln_kernel,
        grid=(rows // tm, f // tf),
        in_specs=[row_blk,
                  pl.BlockSpec((None, d, tf), lambda i, j: (layer, 0, j)),
                  pl.BlockSpec((None, d, tf), lambda i, j: (layer, 0, j)),
                  pl.BlockSpec((None, tf, d), lambda i, j: (layer, j, 0)),
                  row_blk, vec, vec],
        out_specs=[row_blk, row_blk],
        out_shape=[jax.ShapeDtypeStruct((rows, d), F32), jax.ShapeDtypeStruct((rows, d), BF16)],
        compiler_params=_cparams("parallel", "arbitrary"),
    )(x16, w1, w3, w2, resid, ln_g.reshape(1, d), ln_b.reshape(1, d))


def _row_copy(src_hbm, dst_vmem, src_row, dst_row, sem):
    return pltpu.make_async_copy(src_hbm.at[pl.ds(src_row, 1), :], dst_vmem.at[pl.ds(dst_row, 1), :], sem)


def _moe_ffn_kernel(be_ref, bv_ref, tok_ref, x_hbm, w1_ref, w3_ref, w2_ref, o32_ref, buf_ref, x16_ref, sem):
    i = pl.program_id(0)
    j = pl.program_id(1)
    nblk = pl.num_programs(0)
    tm = o32_ref.shape[0]
    valid = bv_ref[i] > 0
    slot = i % 2

    def gather(blk, slt, wait):
        def body(r, carry):
            cp = _row_copy(x_hbm, buf_ref.at[slt], tok_ref[blk * tm + r], r, sem.at[slt])
            if wait:
                cp.wait()
            else:
                cp.start()
            return carry
        lax.fori_loop(0, tm, body, 0, unroll=4)

    @pl.when((j == 0) & (i == 0) & valid)
    def _():
        gather(0, 0, wait=False)

    @pl.when((j == 0) & (i + 1 < nblk))
    def _():
        @pl.when(bv_ref[i + 1] > 0)
        def _():
            gather(i + 1, 1 - slot, wait=False)

    @pl.when((j == 0) & valid)
    def _():
        gather(i, slot, wait=True)
        x16_ref[...] = buf_ref[slot].astype(BF16)

    @pl.when(valid)
    def _():
        _swiglu_accumulate(x16_ref, w1_ref, w3_ref, w2_ref, o32_ref, j)

    @pl.when(jnp.logical_not(valid) & (j == 0))
    def _():
        o32_ref[...] = jnp.zeros_like(o32_ref)


def _moe_ffn(x32, row_token, w1, w3, w2, layer, blk_expert, blk_valid, tm, tf=512):
    n_rows = row_token.shape[0]
    d = x32.shape[1]
    nj = w1.shape[3] // tf

    def jeff(i, j, bv):
        return jnp.where(bv[i] > 0, j, nj - 1)

    return pl.pallas_call(
        _moe_ffn_kernel,
        grid_spec=pltpu.PrefetchScalarGridSpec(
            num_scalar_prefetch=3, grid=(n_rows // tm, nj),
            in_specs=[pl.BlockSpec(memory_space=pl.ANY),
                      pl.BlockSpec((None, None, d, tf), lambda i, j, be, bv, tok: (layer, be[i], 0, jeff(i, j, bv))),
                      pl.BlockSpec((None, None, d, tf), lambda i, j, be, bv, tok: (layer, be[i], 0, jeff(i, j, bv))),
                      pl.BlockSpec((None, None, tf, d), lambda i, j, be, bv, tok: (layer, be[i], jeff(i, j, bv), 0))],
            out_specs=pl.BlockSpec((tm, d), lambda i, j, be, bv, tok: (i, 0)),
            scratch_shapes=[pltpu.VMEM((2, tm, d), F32), pltpu.VMEM((tm, d), BF16),
                            pltpu.SemaphoreType.DMA((2,))]),
        out_shape=jax.ShapeDtypeStruct((n_rows, d), F32),
        compiler_params=_cparams("arbitrary", "arbitrary"),
    )(blk_expert, blk_valid, row_token, x32, w1, w3, w2)


def _retention_kernel(lg_ref, q_ref, k_ref, v_ref, g_ref, cos_ref, sin_ref, nw_ref, o_ref, state_ref):
    c = RET_CHUNK
    hp = pl.program_id(1)

    @pl.when(pl.program_id(2) == 0)
    def _():
        state_ref[...] = jnp.zeros_like(state_ref)

    cos = cos_ref[...]
    sin = sin_ref[...]
    t = lax.broadcasted_iota(jnp.int32, (c, 1), 0).astype(F32)
    tt = lax.broadcasted_iota(jnp.int32, (c, c), 0)
    ss = lax.broadcasted_iota(jnp.int32, (c, c), 1)
    causal = tt >= ss
    lag = (tt - ss).astype(F32)
    for hh in range(RET_HPS):
        qk = slice(hh * RET_DK, (hh + 1) * RET_DK)
        vv = slice(hh * RET_DV, (hh + 1) * RET_DV)
        lg = lg_ref[hp * RET_HPS + hh] * LOG2_E
        q = q_ref[:, qk]
        k = k_ref[:, qk]
        qr = q * cos + pltpu.roll(q, RET_DK // 2, 1) * sin
        kr = (k * cos + pltpu.roll(k, RET_DK // 2, 1) * sin) * (RET_DK ** -0.5)
        decay = jnp.exp2(jnp.where(causal, lag * lg, NEG_BIG))
        v = v_ref[:, vv].astype(BF16)
        state = state_ref[hh]
        scores = _bdot_nt(qr, kr) * decay
        y = jnp.dot(scores.astype(BF16), v, preferred_element_type=F32)
        y = y + _bdot(qr * jnp.exp2((t + 1.0) * lg), state)
        k_dec = kr * jnp.exp2((float(c - 1) - t) * lg)
        state_ref[hh] = jnp.exp2(jnp.full((1, 1), float(c), F32) * lg) * state + _bdot_tn(k_dec, v)
        mu = jnp.mean(y, axis=-1, keepdims=True)
        yc = y - mu
        var = jnp.mean(yc * yc, axis=-1, keepdims=True)
        yn = yc * lax.rsqrt(var + LN_EPS) * nw_ref[:, vv]
        o_ref[:, vv] = (yn * _silu(g_ref[:, vv])).astype(o_ref.dtype)


def _retention(proj, log_gamma, cos_t, sin_t, norm_w):
    b, s, _ = proj.shape
    c = RET_CHUNK
    qw = RET_HPS * RET_DK
    vw = RET_HPS * RET_DV
    kb = RET_QK // qw
    vb = 2 * RET_QK // vw
    gb = vb + RET_V // vw
    return pl.pallas_call(
        _retention_kernel,
        grid_spec=pltpu.PrefetchScalarGridSpec(
            num_scalar_prefetch=1, grid=(b, RET_HEADS // RET_HPS, s // c),
            in_specs=[pl.BlockSpec((None, c, qw), lambda bi, h, ci, lg: (bi, ci, h)),
                      pl.BlockSpec((None, c, qw), lambda bi, h, ci, lg: (bi, ci, kb + h)),
                      pl.BlockSpec((None, c, vw), lambda bi, h, ci, lg: (bi, ci, vb + h)),
                      pl.BlockSpec((None, c, vw), lambda bi, h, ci, lg: (bi, ci, gb + h)),
                      pl.BlockSpec((c, RET_DK), lambda bi, h, ci, lg: (ci, 0)),
                      pl.BlockSpec((c, RET_DK), lambda bi, h, ci, lg: (ci, 0)),
                      pl.BlockSpec((1, vw), lambda bi, h, ci, lg: (0, h))],
            out_specs=pl.BlockSpec((None, c, vw), lambda bi, h, ci, lg: (bi, ci, h)),
            scratch_shapes=[pltpu.VMEM((RET_HPS, RET_DK, RET_DV), F32)]),
        out_shape=jax.ShapeDtypeStruct((b, s, RET_V), BF16),
        compiler_params=_cparams("parallel", "parallel", "arbitrary"),
    )(log_gamma, proj, proj, proj, proj, cos_t, sin_t, norm_w.reshape(1, RET_V))


def _softplus(x):
    return jnp.maximum(x, 0.0) + jnp.log1p(jnp.exp(-jnp.abs(x)))


def _ssd_kernel(z_ref, xs_ref, bm_ref, cm_ref, dt_ref, cwx_ref, cwb_ref, cwc_ref,
                cbx_ref, cbb_ref, cbc_ref, dtb_ref, alog_ref, dsk_ref, nw_ref,
                o_ref, state_ref, ext_ref):
    c = SSD_CHUNK
    gw = SSD_GW
    n = SSD_N
    pad = SUBLANES

    @pl.when(pl.program_id(2) == 0)
    def _():
        state_ref[...] = jnp.zeros_like(state_ref)
        ext_ref[0:pad, :] = jnp.zeros((pad, gw + 2 * n), F32)

    ext_ref[pad:pad + c, 0:gw] = xs_ref[...]
    ext_ref[pad:pad + c, gw:gw + n] = bm_ref[...]
    ext_ref[pad:pad + c, gw + n:gw + 2 * n] = cm_ref[...]
    cw = jnp.concatenate([cwx_ref[...], cwb_ref[...], cwc_ref[...]], axis=1)
    cb = jnp.concatenate([cbx_ref[...], cbb_ref[...], cbc_ref[...]], axis=1)
    conv = cb + cw[SSD_CONV - 1:SSD_CONV, :] * ext_ref[pad:pad + c, :]
    for j in range(SSD_CONV - 1):
        off = pad - (SSD_CONV - 1) + j
        conv = conv + cw[j:j + 1, :] * ext_ref[off:off + c, :]
    tail = ext_ref[c:c + pad, :]
    ext_ref[0:pad, :] = tail
    conv = _silu(conv)
    xs = conv[:, 0:gw]
    bm = conv[:, gw:gw + n]
    cm = conv[:, gw + n:gw + 2 * n]

    dt = _softplus(dt_ref[...] + dtb_ref[...])
    log2_a = -dt * (jnp.exp(alog_ref[...]) * LOG2_E)
    cum = _cumsum_rows(log2_a)
    cum_t = cum.T
    last = cum[c - 1:c, :]
    e_cum = jnp.exp2(cum)
    e_rem = jnp.exp2(last - cum)
    e_last = jnp.exp2(last)

    tt = lax.broadcasted_iota(jnp.int32, (c, c), 0)
    ss = lax.broadcasted_iota(jnp.int32, (c, c), 1)
    causal = tt >= ss
    lane = lax.broadcasted_iota(jnp.int32, (c, 2 * SSD_P), 1)
    first = lane < SSD_P
    lane1 = lax.broadcasted_iota(jnp.int32, (1, 2 * SSD_P), 1)
    first1 = lane1 < SSD_P

    state = state_ref[...]
    gmat = _bdot_nt(cm, bm)
    cross = _bdot(cm, state)
    ys, xws, els = [], [], []
    for p in range(SSD_HPG // 2):
        ha, hb = 2 * p, 2 * p + 1
        lo, hi = p * 2 * SSD_P, (p + 1) * 2 * SSD_P
        xs_p = xs[:, lo:hi]
        xdt = xs_p * jnp.where(first, dt[:, ha:ha + 1], dt[:, hb:hb + 1])
        xdt16 = xdt.astype(BF16)
        dec_a = jnp.exp2(jnp.where(causal, cum[:, ha:ha + 1] - cum_t[ha:ha + 1, :], NEG_BIG))
        dec_b = jnp.exp2(jnp.where(causal, cum[:, hb:hb + 1] - cum_t[hb:hb + 1, :], NEG_BIG))
        ya = jnp.dot((gmat * dec_a).astype(BF16), xdt16, preferred_element_type=F32)
        yb = jnp.dot((gmat * dec_b).astype(BF16), xdt16, preferred_element_type=F32)
        y = jnp.where(first, ya, yb)
        y = y + cross[:, lo:hi] * jnp.where(first, e_cum[:, ha:ha + 1], e_cum[:, hb:hb + 1])
        y = y + dsk_ref[:, lo:hi] * xs_p
        ys.append(y)
        xws.append(xdt * jnp.where(first, e_rem[:, ha:ha + 1], e_rem[:, hb:hb + 1]))
        els.append(jnp.where(first1, e_last[:, ha:ha + 1], e_last[:, hb:hb + 1]))
    y = jnp.concatenate(ys, axis=1)
    xw = jnp.concatenate(xws, axis=1)
    el = jnp.concatenate(els, axis=1)
    state_ref[...] = state * el + _bdot_tn(bm, xw)

    yz = y * _silu(z_ref[...])
    ms = jnp.mean(yz * yz, axis=-1, keepdims=True)
    o_ref[...] = (yz * lax.rsqrt(ms + LN_EPS) * nw_ref[...]).astype(o_ref.dtype)


def _ssd(proj, dt_raw, conv_w, conv_b, dtb, alog, dskip, norm_w):
    b, s, _ = proj.shape
    c = SSD_CHUNK
    gw = SSD_GW
    n = SSD_N
    z0 = (2 * RET_QK + 2 * RET_V) // gw
    x0 = z0 + SSD_DI // gw
    b0 = (2 * RET_QK + 2 * RET_V + SSD_DI + SSD_DI) // n
    c0 = b0 + SSD_G
    wb0 = SSD_DI // n
    wc0 = wb0 + SSD_G
    im3 = lambda f: (lambda bi, g, ci: f(bi, g, ci))
    return pl.pallas_call(
        _ssd_kernel,
        grid=(b, SSD_G, s // c),
        in_specs=[pl.BlockSpec((None, c, gw), im3(lambda bi, g, ci: (bi, ci, z0 + g))),
                  pl.BlockSpec((None, c, gw), im3(lambda bi, g, ci: (bi, ci, x0 + g))),
                  pl.BlockSpec((None, c, n), im3(lambda bi, g, ci: (bi, ci, b0 + g))),
                  pl.BlockSpec((None, c, n), im3(lambda bi, g, ci: (bi, ci, c0 + g))),
                  pl.BlockSpec((None, c, LANES), im3(lambda bi, g, ci: (bi, ci, g))),
                  pl.BlockSpec((SSD_CONV, gw), im3(lambda bi, g, ci: (0, g))),
                  pl.BlockSpec((SSD_CONV, n), im3(lambda bi, g, ci: (0, wb0 + g))),
                  pl.BlockSpec((SSD_CONV, n), im3(lambda bi, g, ci: (0, wc0 + g))),
                  pl.BlockSpec((1, gw), im3(lambda bi, g, ci: (0, g))),
                  pl.BlockSpec((1, n), im3(lambda bi, g, ci: (0, wb0 + g))),
                  pl.BlockSpec((1, n), im3(lambda bi, g, ci: (0, wc0 + g))),
                  pl.BlockSpec((1, LANES), im3(lambda bi, g, ci: (0, g))),
                  pl.BlockSpec((1, LANES), im3(lambda bi, g, ci: (0, g))),
                  pl.BlockSpec((1, gw), im3(lambda bi, g, ci: (0, g))),
                  pl.BlockSpec((1, gw), im3(lambda bi, g, ci: (0, g)))],
        out_specs=pl.BlockSpec((None, c, gw), im3(lambda bi, g, ci: (bi, ci, g))),
        out_shape=jax.ShapeDtypeStruct((b, s, SSD_DI), BF16),
        scratch_shapes=[pltpu.VMEM((n, gw), F32), pltpu.VMEM((c + SUBLANES, gw + 2 * n), F32)],
        compiler_params=_cparams("parallel", "parallel", "arbitrary"),
    )(proj, proj, proj, proj, dt_raw, conv_w, conv_w, conv_w,
      conv_b.reshape(1, SSD_XBC), conv_b.reshape(1, SSD_XBC), conv_b.reshape(1, SSD_XBC),
      dtb, alog, dskip, norm_w.reshape(1, SSD_DI))


def _group_lanes(v):
    v = v.reshape(SSD_G, SSD_HPG)
    return jnp.pad(v, ((0, 0), (0, LANES - SSD_HPG))).reshape(1, SSD_G * LANES)


def _hgrn2_masks():
    c = HG_CHUNK
    t = lax.broadcasted_iota(jnp.int32, (c, c), 0)
    s = lax.broadcasted_iota(jnp.int32, (c, c), 1)
    masks = []
    blk = c
    while blk > SUBLANES:
        half = blk // 2
        m = (t // blk == s // blk) & ((t % blk) >= half) & ((s % blk) < half)
        masks.append((blk, m))
        blk = half
    return masks


def _hgrn2_kernel(q_ref, f_ref, i_ref, g_ref, lbl_ref, nw_ref, o_ref, state_ref, *, layer):
    c = HG_CHUNK
    dk = HG_DK

    @pl.when(pl.program_id(2) == 0)
    def _():
        state_ref[...] = jnp.zeros_like(state_ref)

    lbl = lbl_ref[...]
    ex = jnp.exp(lbl - jnp.max(lbl, axis=0, keepdims=True))
    pr = ex / jnp.sum(ex, axis=0, keepdims=True)
    lb_all = jnp.sum(pr[1:layer + 1, :], axis=0, keepdims=True)
    nw_all = nw_ref[...]

    masks = _hgrn2_masks()
    row = lax.broadcasted_iota(jnp.int32, (c, dk), 0)
    sub = row % SUBLANES
    trow = lax.broadcasted_iota(jnp.int32, (c, c), 0)
    scol = lax.broadcasted_iota(jnp.int32, (c, c), 1)
    tile_base = (trow // SUBLANES) * SUBLANES

    def sub_bcast(x, j):
        x3 = x.reshape(c // SUBLANES, SUBLANES, dk)
        return jnp.broadcast_to(x3[:, j:j + 1, :], x3.shape).reshape(c, dk)

    def head_chunk(r0, hh):
        cols = slice(hh * dk, (hh + 1) * dk)
        lb = lb_all[:, cols]
        fr = f_ref[pl.ds(r0, c), cols]
        sg = _sigmoid(fr)
        log2_f = jnp.log2(lb + (1.0 - lb) * sg)
        kk = (1.0 - lb) * (1.0 - sg)
        qq = _silu(q_ref[pl.ds(r0, c), cols])
        v16 = i_ref[pl.ds(r0, c), cols].astype(BF16)
        cum = _cumsum_rows(log2_f)

        scores = jnp.zeros((c, c), F32)
        for blk, m in masks:
            half = blk // 2
            refv = jnp.concatenate(
                [jnp.broadcast_to(cum[nb * blk + half - 1:nb * blk + half, :], (blk, dk))
                 for nb in range(c // blk)], axis=0)
            e = jnp.exp2(-jnp.abs(cum - refv))
            scores = scores + jnp.where(m, _bdot_nt(qq * e, kk * e), 0.0)
        for j in range(SUBLANES):
            e = jnp.exp2(jnp.where(sub >= j, cum - sub_bcast(cum, j), NEG_BIG))
            col = jnp.sum(qq * e * sub_bcast(kk, j), axis=-1, keepdims=True)
            scores = scores + jnp.where(scol == tile_base + j, col, 0.0)

        state_t = state_ref[hh]
        y = jnp.dot(scores.astype(BF16), v16, preferred_element_type=F32)
        y = y + _bdot_nt(qq * jnp.exp2(cum), state_t)
        last = cum[c - 1:c, :]
        k_dec = kk * jnp.exp2(last - cum)
        state_ref[hh] = state_t * jnp.exp2(last) + _bdot_tn(v16, k_dec)
        ms = jnp.mean(y * y, axis=-1, keepdims=True)
        out = y * lax.rsqrt(ms + LN_EPS) * nw_all[:, cols] * _silu(g_ref[pl.ds(r0, c), cols])
        o_ref[pl.ds(r0, c), cols] = out.astype(o_ref.dtype)

    def chunk(ci, carry):
        r0 = pl.multiple_of(ci * c, c)
        for hh in range(HG_HPS):
            head_chunk(r0, hh)
        return carry

    lax.fori_loop(0, HG_ROWS // c, chunk, 0)


def _hgrn2(proj, lb_logits, norm_w, layer):
    b, s, _ = proj.shape
    r = HG_ROWS
    w = HG_HPS * HG_DK
    nb = HG_HEADS // HG_HPS
    return pl.pallas_call(
        functools.partial(_hgrn2_kernel, layer=layer),
        grid=(b, nb, s // r),
        in_specs=[pl.BlockSpec((None, r, w), lambda bi, hi, ri: (bi, ri, hi)),
                  pl.BlockSpec((None, r, w), lambda bi, hi, ri: (bi, ri, nb + hi)),
                  pl.BlockSpec((None, r, w), lambda bi, hi, ri: (bi, ri, 2 * nb + hi)),
                  pl.BlockSpec((None, r, w), lambda bi, hi, ri: (bi, ri, 3 * nb + hi)),
                  pl.BlockSpec((DEPTH, w), lambda bi, hi, ri: (0, hi)),
                  pl.BlockSpec((1, w), lambda bi, hi, ri: (0, hi))],
        out_specs=pl.BlockSpec((None, r, w), lambda bi, hi, ri: (bi, ri, hi)),
        out_shape=jax.ShapeDtypeStruct((b, s, HG_HEADS * HG_DK), BF16),
        scratch_shapes=[pltpu.VMEM((HG_HPS, HG_DK, HG_DK), F32)],
        compiler_params=_cparams("parallel", "parallel", "arbitrary"),
    )(proj, proj, proj, proj, lb_logits, norm_w.reshape(1, HG_HEADS * HG_DK))


def _router_kernel(x_ref, w_ref, idx_ref, gate_ref, cnt_ref, carry_ref):
    tm = x_ref.shape[0]

    @pl.when(pl.program_id(0) == 0)
    def _():
        carry_ref[...] = jnp.zeros_like(carry_ref)

    logits = jnp.dot(x_ref[...], w_ref[...], preferred_element_type=F32,
                     precision=lax.Precision.HIGHEST)
    lane = lax.broadcasted_iota(jnp.int32, (tm, LANES), 1)
    lane_f = lane.astype(F32)
    lg = jnp.where(lane < N_EXPERTS, logits, NEG_BIG)
    m1 = jnp.max(lg, axis=-1, keepdims=True)
    i1 = jnp.min(jnp.where(lg == m1, lane_f, float(LANES)), axis=-1, keepdims=True).astype(jnp.int32)
    lg2 = jnp.where(lane == i1, NEG_BIG, lg)
    m2 = jnp.max(lg2, axis=-1, keepdims=True)
    i2 = jnp.min(jnp.where(lg2 == m2, lane_f, float(LANES)), axis=-1, keepdims=True).astype(jnp.int32)
    e2 = jnp.exp(m2 - m1)
    den = 1.0 + e2
    gate_ref[...] = jnp.where(lane == 0, 1.0 / den, jnp.where(lane == 1, e2 / den, 0.0))

    sel1 = lane == i1
    sel2 = lane == i2
    assign = (sel1 | sel2).astype(BF16)
    tt = lax.broadcasted_iota(jnp.int32, (tm, tm), 0)
    ss = lax.broadcasted_iota(jnp.int32, (tm, tm), 1)
    before = (ss < tt).astype(BF16)
    base = carry_ref[...] + jnp.dot(before, assign, preferred_element_type=F32)
    r1 = jnp.sum(jnp.where(sel1, base, 0.0), axis=-1, keepdims=True).astype(jnp.int32)
    r2 = jnp.sum(jnp.where(sel2, base, 0.0), axis=-1, keepdims=True).astype(jnp.int32)
    idx_ref[...] = jnp.where(lane == 0, i1, jnp.where(lane == 1, i2,
                             jnp.where(lane == 2, r1, jnp.where(lane == 3, r2, 0))))
    carry_ref[...] += jnp.sum(assign.astype(F32), axis=0, keepdims=True)
    cnt_ref[...] = carry_ref[...]


def _router(x32, w_router, tm=512):
    n, d = x32.shape
    wr = jnp.pad(w_router, ((0, 0), (0, LANES - N_EXPERTS)))
    return pl.pallas_call(
        _router_kernel,
        grid=(n // tm,),
        in_specs=[pl.BlockSpec((tm, d), lambda i: (i, 0)),
                  pl.BlockSpec((d, LANES), lambda i: (0, 0))],
        out_specs=[pl.BlockSpec((tm, LANES), lambda i: (i, 0)),
                   pl.BlockSpec((tm, LANES), lambda i: (i, 0)),
                   pl.BlockSpec((1, LANES), lambda i: (0, 0))],
        out_shape=[jax.ShapeDtypeStruct((n, LANES), jnp.int32),
                   jax.ShapeDtypeStruct((n, LANES), F32),
                   jax.ShapeDtypeStruct((1, LANES), F32)],
        scratch_shapes=[pltpu.VMEM((1, LANES), F32)],
        compiler_params=_cparams("arbitrary"),
    )(x32, wr)


def _combine_kernel(pos_ref, y_hbm, gate_ref, x_ref, g_ref, b_ref, o32_ref, o16_ref, buf_ref, sem):
    i = pl.program_id(0)
    nblk = pl.num_programs(0)
    tc = x_ref.shape[0]
    slot = i % 2

    def gather(blk, slt, wait):
        def body(r, carry):
            for k in range(2):
                cp = _row_copy(y_hbm, buf_ref.at[slt, k], pos_ref[2 * (blk * tc + r) + k], r, sem.at[slt])
                if wait:
                    cp.wait()
                else:
                    cp.start()
            return carry
        lax.fori_loop(0, tc, body, 0, unroll=2)

    @pl.when(i == 0)
    def _():
        gather(0, 0, wait=False)

    @pl.when(i + 1 < nblk)
    def _():
        gather(i + 1, 1 - slot, wait=False)

    gather(i, slot, wait=True)
    gates = gate_ref[...]
    ffn = gates[:, 0:1] * buf_ref[slot, 0] + gates[:, 1:2] * buf_ref[slot, 1]
    out = _layer_norm_rows(ALPHA * x_ref[...] + ffn, g_ref[...], b_ref[...])
    o32_ref[...] = out
    o16_ref[...] = out.astype(BF16)


def _combine(pos, yr, gates, x32, ln_g, ln_b, tc=GATHER_ROWS):
    n, d = x32.shape
    return pl.pallas_call(
        _combine_kernel,
        grid_spec=pltpu.PrefetchScalarGridSpec(
            num_scalar_prefetch=1, grid=(n // tc,),
            in_specs=[pl.BlockSpec(memory_space=pl.ANY),
                      pl.BlockSpec((tc, LANES), lambda i, p: (i, 0)),
                      pl.BlockSpec((tc, d), lambda i, p: (i, 0)),
                      pl.BlockSpec((1, d), lambda i, p: (0, 0)),
                      pl.BlockSpec((1, d), lambda i, p: (0, 0))],
            out_specs=[pl.BlockSpec((tc, d), lambda i, p: (i, 0)),
                       pl.BlockSpec((tc, d), lambda i, p: (i, 0))],
            scratch_shapes=[pltpu.VMEM((2, 2, tc, d), F32), pltpu.SemaphoreType.DMA((2,))]),
        out_shape=[jax.ShapeDtypeStruct((n, d), F32), jax.ShapeDtypeStruct((n, d), BF16)],
        compiler_params=_cparams("arbitrary"),
    )(pos, yr, gates, x32, ln_g.reshape(1, d), ln_b.reshape(1, d))


def _moe(x32, w_router, w1, w3, w2, layer, ln_g, ln_b):
    n, d = x32.shape
    tm = MOE_TM
    idx, gates, counts = _router(x32, w_router)
    counts = counts[0, :N_EXPERTS].astype(jnp.int32)
    expert = idx[:, 0:2]
    rank = idx[:, 2:4]
    padded = (counts + tm - 1) // tm * tm
    padded_end = jnp.cumsum(padded)
    start_padded = padded_end - padded
    dest = start_padded[expert] + rank
    n_rows = 2 * n + N_EXPERTS * tm
    n_blocks = n_rows // tm
    token = jnp.broadcast_to(jnp.arange(n, dtype=jnp.int32)[:, None], (n, 2))
    row_token = jnp.zeros((n_rows,), jnp.int32).at[dest.reshape(-1)].set(token.reshape(-1))
    block_start = jnp.arange(n_blocks, dtype=jnp.int32) * tm
    blk_valid = (block_start < padded_end[-1]).astype(jnp.int32)
    last_expert = jnp.minimum(jnp.searchsorted(padded_end, padded_end[-1] - 1, side="right"), N_EXPERTS - 1)
    blk_expert = jnp.minimum(jnp.searchsorted(padded_end, block_start, side="right"), N_EXPERTS - 1)
    blk_expert = jnp.where(blk_valid > 0, blk_expert, last_expert).astype(jnp.int32)
    yr = _moe_ffn(x32, row_token, w1, w3, w2, layer, blk_expert, blk_valid, tm)
    return _combine(dest.reshape(-1).astype(jnp.int32), yr, gates, x32, ln_g, ln_b)


def _rope_tables(s):
    inv_freq = ROPE_BASE ** (-jnp.arange(0, RET_DK, 2, dtype=F32) / RET_DK)
    ang = jnp.arange(s, dtype=F32)[:, None] * inv_freq[None, :]
    cos = jnp.cos(ang)
    sin = jnp.sin(ang)
    return jnp.concatenate([cos, cos], axis=-1), jnp.concatenate([-sin, sin], axis=-1)


def _even_layer(x32, x16, b, s, j, w_in, ret_norm_w, conv_w, conv_b, dt_bias, a_log, d_skip,
                ssd_norm_w, w_out, ln1_g, ln1_b, w1, w3, w2, ln2_g, ln2_b):
    n = b * s
    w_dt = w_in[j, :, EV_MAIN:].reshape(D_MODEL, SSD_G, SSD_HPG)
    w_dt = jnp.pad(w_dt, ((0, 0), (0, 0), (0, LANES - SSD_HPG))).reshape(1, D_MODEL, SSD_G * LANES)
    proj = _matmul(x16, w_in, j, 1024, 1024, F32, n=EV_MAIN).reshape(b, s, EV_MAIN)
    dt_raw = _matmul(x16, w_dt, 0, 1024, SSD_G * LANES, F32).reshape(b, s, SSD_G * LANES)
    log_gamma = jnp.log1p(-jnp.exp2(-5.0 - jnp.arange(RET_HEADS, dtype=F32)))
    cos_t, sin_t = _rope_tables(s)
    ret = _retention(proj, log_gamma, cos_t, sin_t, ret_norm_w)
    ssd = _ssd(proj, dt_raw, conv_w, conv_b, _group_lanes(dt_bias), _group_lanes(a_log),
               jnp.repeat(d_skip, SSD_P).reshape(1, SSD_DI), ssd_norm_w)
    x32, x16 = _matmul_ln([ret.reshape(n, RET_V), ssd.reshape(n, SSD_DI)], w_out, j, x32, ln1_g, ln1_b)
    return _ffn_ln(x16, w1, w3, w2, j, x32, ln2_g, ln2_b)


def _odd_layer(x32, x16, b, s, layer, w_in, lb_logits, hg_norm_w, w_out, ln1_g, ln1_b,
               w_router, w1, w3, w2, ln2_g, ln2_b):
    n = b * s
    j = layer // 2
    proj = _matmul(x16, w_in, j, 1024, 1024, F32).reshape(b, s, -1)
    o = _hgrn2(proj, lb_logits, hg_norm_w, layer)
    x32, x16 = _matmul_ln([o.reshape(n, -1)], w_out, j, x32, ln1_g, ln1_b)
    return _moe(x32, w_router, w1, w3, w2, j, ln2_g, ln2_b)


def kernel(x, ev_w_in, ev_ret_norm_w, ev_conv_w, ev_conv_b, ev_dt_bias, ev_a_log, ev_d_skip,
           ev_ssd_norm_w, ev_w_out, ev_ln1_g, ev_ln1_b, ffn_w1, ffn_w3, ffn_w2, ev_ln2_g, ev_ln2_b,
           od_w_in, hg_lb_logits, od_hg_norm_w, od_w_out, od_ln1_g, od_ln1_b, moe_router,
           moe_w1, moe_w3, moe_w2, od_ln2_g, od_ln2_b):
    b, s, d = x.shape
    x32 = x.reshape(b * s, d)
    x16 = x32.astype(BF16)
    ev_w_in, ev_w_out, ffn_w1, ffn_w3, ffn_w2, od_w_in, od_w_out, moe_w1, moe_w3, moe_w2 = (
        w.astype(BF16) for w in (ev_w_in, ev_w_out, ffn_w1, ffn_w3, ffn_w2, od_w_in, od_w_out,
                                 moe_w1, moe_w3, moe_w2))
    for layer in range(DEPTH):
        j = layer // 2
        if layer % 2 == 0:
            x32, x16 = _even_layer(x32, x16, b, s, j, ev_w_in, ev_ret_norm_w[j], ev_conv_w[j], ev_conv_b[j],
                                   ev_dt_bias[j], ev_a_log[j], ev_d_skip[j], ev_ssd_norm_w[j], ev_w_out,
                                   ev_ln1_g[j], ev_ln1_b[j], ffn_w1, ffn_w3, ffn_w2,
                                   ev_ln2_g[j], ev_ln2_b[j])
        else:
            x32, x16 = _odd_layer(x32, x16, b, s, layer, od_w_in, hg_lb_logits, od_hg_norm_w[j],
                                  od_w_out, od_ln1_g[j], od_ln1_b[j], moe_router[j],
                                  moe_w1, moe_w3, moe_w2, od_ln2_g[j], od_ln2_b[j])
    return x32.reshape(b, s, d)
```

```python
import functools
import math

import numpy as np
import jax
import jax.numpy as jnp
from jax import lax
from jax.experimental import pallas as pl
from jax.experimental.pallas import tpu as pltpu

F32 = jnp.float32
BF16 = jnp.bfloat16

D_MODEL = 2048
DEPTH = 4
LN_EPS = 1e-5
RET_HEADS = 8
RET_DK = 128
RET_DV = 256
RET_QK = RET_HEADS * RET_DK
RET_V = RET_HEADS * RET_DV
ROPE_BASE = 10000.0
SSD_DI = 2048
SSD_P = 64
SSD_HEADS = 32
SSD_G = 4
SSD_N = 128
SSD_CONV = 4
SSD_XBC = SSD_DI + 2 * SSD_G * SSD_N
SSD_HPG = SSD_HEADS // SSD_G
SSD_GW = SSD_HPG * SSD_P
HG_DK = 128
HG_HEADS = 16
FFN_DIM = 5632
N_EXPERTS = 8
EV_MAIN = 2 * RET_QK + 2 * RET_V + SSD_DI + SSD_XBC
ALPHA = (2.0 * DEPTH) ** 0.25

LANES = 128
SUBLANES = 8
VMEM_LIMIT = 52 * 1024 * 1024
NEG_BIG = -1e30
LOG2_E = math.log2(math.e)

RET_CHUNK = 256
RET_HPS = 4
SSD_CHUNK = 256
HG_CHUNK = 64
HG_ROWS = 512
HG_HPS = 16
MOE_TM = 512
GATHER_ROWS = 256
FFN_SPLIT = 2


def _cparams(*sem):
    return pltpu.CompilerParams(dimension_semantics=sem, vmem_limit_bytes=VMEM_LIMIT)


def _sigmoid(x):
    return 0.5 * jnp.tanh(0.5 * x) + 0.5


def _silu(x):
    return x * _sigmoid(x)


def _bdot(a, b):
    return jnp.dot(a.astype(BF16), b.astype(BF16), preferred_element_type=F32)


def _bdot_nt(a, b):
    return lax.dot_general(a.astype(BF16), b.astype(BF16), (((1,), (1,)), ((), ())),
                           preferred_element_type=F32)


def _bdot_tn(a, b):
    return lax.dot_general(a.astype(BF16), b.astype(BF16), (((0,), (0,)), ((), ())),
                           preferred_element_type=F32)


def _layer_norm_rows(y, g, b):
    mu = jnp.mean(y, axis=-1, keepdims=True)
    yc = y - mu
    var = jnp.mean(yc * yc, axis=-1, keepdims=True)
    return yc * lax.rsqrt(var + LN_EPS) * g + b


def _cumsum_rows(x):
    n = x.shape[0]
    row = lax.broadcasted_iota(jnp.int32, x.shape, 0)
    shift = 1
    while shift < n:
        x = x + jnp.where(row >= shift, pltpu.roll(x, shift, 0), 0.0)
        shift *= 2
    return x


def _mm_kernel(x_ref, w_ref, o_ref):
    o_ref[...] = jnp.dot(x_ref[...], w_ref[...], preferred_element_type=F32).astype(o_ref.dtype)


def _matmul(x, w, layer, tm, tn, out_dtype, n=None):
    m, k = x.shape
    n = w.shape[2] if n is None else n
    return pl.pallas_call(
        _mm_kernel,
        grid=(m // tm, n // tn),
        in_specs=[pl.BlockSpec((tm, k), lambda i, j: (i, 0)),
                  pl.BlockSpec((None, k, tn), lambda i, j: (layer, 0, j))],
        out_specs=pl.BlockSpec((tm, tn), lambda i, j: (i, j)),
        out_shape=jax.ShapeDtypeStruct((m, n), out_dtype),
        compiler_params=_cparams("parallel", "parallel"),
    )(x, w)


def _mm_ln_kernel(*refs, n_in, kpi):
    a_refs = refs[:n_in]
    w_ref, r_ref, g_ref, b_ref, o32_ref, o16_ref, acc_ref = refs[n_in:]
    k = pl.program_id(1)

    @pl.when(k == 0)
    def _():
        acc_ref[...] = jnp.zeros_like(acc_ref)

    def accumulate(a_ref):
        rows = a_ref.shape[0] // FFN_SPLIT
        for r in range(FFN_SPLIT):
            sl = slice(r * rows, (r + 1) * rows)
            acc_ref[sl, :] += jnp.dot(a_ref[sl, :], w_ref[...], preferred_element_type=F32)

    if n_in == 1:
        accumulate(a_refs[0])
    else:
        for j in range(n_in):
            @pl.when((k >= j * kpi) & (k < (j + 1) * kpi))
            def _(j=j):
                accumulate(a_refs[j])

    @pl.when(k == n_in * kpi - 1)
    def _():
        y = ALPHA * r_ref[...] + acc_ref[...]
        out = _layer_norm_rows(y, g_ref[...], b_ref[...])
        o32_ref[...] = out
        o16_ref[...] = out.astype(BF16)


def _matmul_ln(a_list, w, layer, resid, ln_g, ln_b, tm=512, tk=1024):
    n_in = len(a_list)
    m, ka = a_list[0].shape
    kpi = ka // tk
    n = w.shape[2]

    def a_map(j):
        return lambda i, k: (i, jnp.clip(k - j * kpi, 0, kpi - 1))

    in_specs = [pl.BlockSpec((tm, tk), a_map(j)) for j in range(n_in)]
    in_specs += [pl.BlockSpec((None, tk, n), lambda i, k: (layer, k, 0)),
                 pl.BlockSpec((tm, n), lambda i, k: (i, 0)),
                 pl.BlockSpec((1, n), lambda i, k: (0, 0)),
                 pl.BlockSpec((1, n), lambda i, k: (0, 0))]
    return pl.pallas_call(
        functools.partial(_mm_ln_kernel, n_in=n_in, kpi=kpi),
        grid=(m // tm, n_in * kpi),
        in_specs=in_specs,
        out_specs=[pl.BlockSpec((tm, n), lambda i, k: (i, 0)),
                   pl.BlockSpec((tm, n), lambda i, k: (i, 0))],
        out_shape=[jax.ShapeDtypeStruct((m, n), F32), jax.ShapeDtypeStruct((m, n), BF16)],
        scratch_shapes=[pltpu.VMEM((tm, n), F32)],
        compiler_params=_cparams("parallel", "arbitrary"),
    )(*a_list, w, resid, ln_g.reshape(1, n), ln_b.reshape(1, n))


def _swiglu_accumulate(x_ref, w1_ref, w3_ref, w2_ref, o32_ref, j):
    @pl.when(j == 0)
    def _():
        o32_ref[...] = jnp.zeros_like(o32_ref)

    rows = x_ref.shape[0] // FFN_SPLIT
    for r in range(FFN_SPLIT):
        sl = slice(r * rows, (r + 1) * rows)
        x16 = x_ref[sl, :]
        h1 = jnp.dot(x16, w1_ref[...], preferred_element_type=F32)
        h3 = jnp.dot(x16, w3_ref[...], preferred_element_type=F32)
        h = (_silu(h1) * h3).astype(BF16)
        o32_ref[sl, :] += jnp.dot(h, w2_ref[...], preferred_element_type=F32)


def _ffn_ln_kernel(x_ref, w1_ref, w3_ref, w2_ref, r_ref, g_ref, b_ref, o32_ref, o16_ref):
    j = pl.program_id(1)
    _swiglu_accumulate(x_ref, w1_ref, w3_ref, w2_ref, o32_ref, j)

    @pl.when(j == pl.num_programs(1) - 1)
    def _():
        y = ALPHA * r_ref[...] + o32_ref[...]
        out = _layer_norm_rows(y, g_ref[...], b_ref[...])
        o32_ref[...] = out
        o16_ref[...] = out.astype(BF16)


def _ffn_ln(x16, w1, w3, w2, layer, resid, ln_g, ln_b, tm=512, tf=512):
    rows, d = x16.shape
    f = w1.shape[2]
    row_blk = pl.BlockSpec((tm, d), lambda i, j: (i, 0))
    vec = pl.BlockSpec((1, d), lambda i, j: (0, 0))
    return pl.pallas_call(
        _ffn_ln_kernel,
        grid=(rows // tm, f // tf),
        in_specs=[row_blk,
                  pl.BlockSpec((None, d, tf), lambda i, j: (layer, 0, j)),
                  pl.BlockSpec((None, d, tf), lambda i, j: (layer, 0, j)),
                  pl.BlockSpec((None, tf, d), lambda i, j: (layer, j, 0)),
                  row_blk, vec, vec],
        out_specs=[row_blk, row_blk],
        out_shape=[jax.ShapeDtypeStruct((rows, d), F32), jax.ShapeDtypeStruct((rows, d), BF16)],
        compiler_params=_cparams("parallel", "arbitrary"),
    )(x16, w1, w3, w2, resid, ln_g.reshape(1, d), ln_b.reshape(1, d))


def _row_copy(src_hbm, dst_vmem, src_row, dst_row, sem):
    return pltpu.make_async_copy(src_hbm.at[pl.ds(src_row, 1), :], dst_vmem.at[pl.ds(dst_row, 1), :], sem)


def _moe_ffn_kernel(be_ref, bv_ref, tok_ref, x_hbm, w1_ref, w3_ref, w2_ref, o32_ref, buf_ref, x16_ref, sem):
    i = pl.program_id(0)
    j = pl.program_id(1)
    nblk = pl.num_programs(0)
    tm = o32_ref.shape[0]
    valid = bv_ref[i] > 0
    slot = i % 2

    def gather(blk, slt, wait):
        def body(r, carry):
            cp = _row_copy(x_hbm, buf_ref.at[slt], tok_ref[blk * tm + r], r, sem.at[slt])
            if wait:
                cp.wait()
            else:
                cp.start()
            return carry
        lax.fori_loop(0, tm, body, 0, unroll=4)

    @pl.when((j == 0) & (i == 0) & valid)
    def _():
        gather(0, 0, wait=False)

    @pl.when((j == 0) & (i + 1 < nblk))
    def _():
        @pl.when(bv_ref[i + 1] > 0)
        def _():
            gather(i + 1, 1 - slot, wait=False)

    @pl.when((j == 0) & valid)
    def _():
        gather(i, slot, wait=True)
        x16_ref[...] = buf_ref[slot].astype(BF16)

    @pl.when(valid)
    def _():
        _swiglu_accumulate(x16_ref, w1_ref, w3_ref, w2_ref, o32_ref, j)

    @pl.when(jnp.logical_not(valid) & (j == 0))
    def _():
        o32_ref[...] = jnp.zeros_like(o32_ref)


def _moe_ffn(x32, row_token, w1, w3, w2, layer, blk_expert, blk_valid, tm, tf=512):
    n_rows = row_token.shape[0]
    d = x32.shape[1]
    nj = w1.shape[3] // tf

    def jeff(i, j, bv):
        return jnp.where(bv[i] > 0, j, nj - 1)

    return pl.pallas_call(
        _moe_ffn_kernel,
        grid_spec=pltpu.PrefetchScalarGridSpec(
            num_scalar_prefetch=3, grid=(n_rows // tm, nj),
            in_specs=[pl.BlockSpec(memory_space=pl.ANY),
                      pl.BlockSpec((None, None, d, tf), lambda i, j, be, bv, tok: (layer, be[i], 0, jeff(i, j, bv))),
                      pl.BlockSpec((None, None, d, tf), lambda i, j, be, bv, tok: (layer, be[i], 0, jeff(i, j, bv))),
                      pl.BlockSpec((None, None, tf, d), lambda i, j, be, bv, tok: (layer, be[i], jeff(i, j, bv), 0))],
            out_specs=pl.BlockSpec((tm, d), lambda i, j, be, bv, tok: (i, 0)),
            scratch_shapes=[pltpu.VMEM((2, tm, d), F32), pltpu.VMEM((tm, d), BF16),
                            pltpu.SemaphoreType.DMA((2,))]),
        out_shape=jax.ShapeDtypeStruct((n_rows, d), F32),
        compiler_params=_cparams("arbitrary", "arbitrary"),
    )(blk_expert, blk_valid, row_token, x32, w1, w3, w2)


def _retention_kernel(lg_ref, q_ref, k_ref, v_ref, g_ref, cos_ref, sin_ref, nw_ref, o_ref, state_ref):
    c = RET_CHUNK
    hp = pl.program_id(1)

    @pl.when(pl.program_id(2) == 0)
    def _():
        state_ref[...] = jnp.zeros_like(state_ref)

    cos = cos_ref[...]
    sin = sin_ref[...]
    t = lax.broadcasted_iota(jnp.int32, (c, 1), 0).astype(F32)
    tt = lax.broadcasted_iota(jnp.int32, (c, c), 0)
    ss = lax.broadcasted_iota(jnp.int32, (c, c), 1)
    causal = tt >= ss
    lag = (tt - ss).astype(F32)
    for hh in range(RET_HPS):
        qk = slice(hh * RET_DK, (hh + 1) * RET_DK)
        vv = slice(hh * RET_DV, (hh + 1) * RET_DV)
        lg = lg_ref[hp * RET_HPS + hh] * LOG2_E
        q = q_ref[:, qk]
        k = k_ref[:, qk]
        qr = q * cos + pltpu.roll(q, RET_DK // 2, 1) * sin
        kr = (k * cos + pltpu.roll(k, RET_DK // 2, 1) * sin) * (RET_DK ** -0.5)
        decay = jnp.exp2(jnp.where(causal, lag * lg, NEG_BIG))
        v = v_ref[:, vv].astype(BF16)
        state = state_ref[hh]
        scores = _bdot_nt(qr, kr) * decay
        y = jnp.dot(scores.astype(BF16), v, preferred_element_type=F32)
        y = y + _bdot(qr * jnp.exp2((t + 1.0) * lg), state)
        k_dec = kr * jnp.exp2((float(c - 1) - t) * lg)
        state_ref[hh] = jnp.exp2(jnp.full((1, 1), float(c), F32) * lg) * state + _bdot_tn(k_dec, v)
        mu = jnp.mean(y, axis=-1, keepdims=True)
        yc = y - mu
        var = jnp.mean(yc * yc, axis=-1, keepdims=True)
        yn = yc * lax.rsqrt(var + LN_EPS) * nw_ref[:, vv]
        o_ref[:, vv] = (yn * _silu(g_ref[:, vv])).astype(o_ref.dtype)


def _retention(proj, log_gamma, cos_t, sin_t, norm_w):
    b, s, _ = proj.shape
    c = RET_CHUNK
    qw = RET_HPS * RET_DK
    vw = RET_HPS * RET_DV
    kb = RET_QK // qw
    vb = 2 * RET_QK // vw
    gb = vb + RET_V // vw
    return pl.pallas_call(
        _retention_kernel,
        grid_spec=pltpu.PrefetchScalarGridSpec(
            num_scalar_prefetch=1, grid=(b, RET_HEADS // RET_HPS, s // c),
            in_specs=[pl.BlockSpec((None, c, qw), lambda bi, h, ci, lg: (bi, ci, h)),
                      pl.BlockSpec((None, c, qw), lambda bi, h, ci, lg: (bi, ci, kb + h)),
                      pl.BlockSpec((None, c, vw), lambda bi, h, ci, lg: (bi, ci, vb + h)),
                      pl.BlockSpec((None, c, vw), lambda bi, h, ci, lg: (bi, ci, gb + h)),
                      pl.BlockSpec((c, RET_DK), lambda bi, h, ci, lg: (ci, 0)),
                      pl.BlockSpec((c, RET_DK), lambda bi, h, ci, lg: (ci, 0)),
                      pl.BlockSpec((1, vw), lambda bi, h, ci, lg: (0, h))],
            out_specs=pl.BlockSpec((None, c, vw), lambda bi, h, ci, lg: (bi, ci, h)),
            scratch_shapes=[pltpu.VMEM((RET_HPS, RET_DK, RET_DV), F32)]),
        out_shape=jax.ShapeDtypeStruct((b, s, RET_V), BF16),
        compiler_params=_cparams("parallel", "parallel", "arbitrary"),
    )(log_gamma, proj, proj, proj, proj, cos_t, sin_t, norm_w.reshape(1, RET_V))


def _softplus(x):
    return jnp.maximum(x, 0.0) + jnp.log1p(jnp.exp(-jnp.abs(x)))


def _ssd_kernel(z_ref, xs_ref, bm_ref, cm_ref, dt_ref, cwx_ref, cwb_ref, cwc_ref,
                cbx_ref, cbb_ref, cbc_ref, dtb_ref, alog_ref, dsk_ref, nw_ref,
                o_ref, state_ref, ext_ref):
    c = SSD_CHUNK
    gw = SSD_GW
    n = SSD_N
    pad = SUBLANES

    @pl.when(pl.program_id(2) == 0)
    def _():
        state_ref[...] = jnp.zeros_like(state_ref)
        ext_ref[0:pad, :] = jnp.zeros((pad, gw + 2 * n), F32)

    ext_ref[pad:pad + c, 0:gw] = xs_ref[...]
    ext_ref[pad:pad + c, gw:gw + n] = bm_ref[...]
    ext_ref[pad:pad + c, gw + n:gw + 2 * n] = cm_ref[...]
    cw = jnp.concatenate([cwx_ref[...], cwb_ref[...], cwc_ref[...]], axis=1)
    cb = jnp.concatenate([cbx_ref[...], cbb_ref[...], cbc_ref[...]], axis=1)
    conv = cb + cw[SSD_CONV - 1:SSD_CONV, :] * ext_ref[pad:pad + c, :]
    for j in range(SSD_CONV - 1):
        off = pad - (SSD_CONV - 1) + j
        conv = conv + cw[j:j + 1, :] * ext_ref[off:off + c, :]
    tail = ext_ref[c:c + pad, :]
    ext_ref[0:pad, :] = tail
    conv = _silu(conv)
    xs = conv[:, 0:gw]
    bm = conv[:, gw:gw + n]
    cm = conv[:, gw + n:gw + 2 * n]

    dt = _softplus(dt_ref[...] + dtb_ref[...])
    log2_a = -dt * (jnp.exp(alog_ref[...]) * LOG2_E)
    cum = _cumsum_rows(log2_a)
    cum_t = cum.T
    last = cum[c - 1:c, :]
    e_cum = jnp.exp2(cum)
    e_rem = jnp.exp2(last - cum)
    e_last = jnp.exp2(last)

    tt = lax.broadcasted_iota(jnp.int32, (c, c), 0)
    ss = lax.broadcasted_iota(jnp.int32, (c, c), 1)
    causal = tt >= ss
    lane = lax.broadcasted_iota(jnp.int32, (c, 2 * SSD_P), 1)
    first = lane < SSD_P
    lane1 = lax.broadcasted_iota(jnp.int32, (1, 2 * SSD_P), 1)
    first1 = lane1 < SSD_P

    state = state_ref[...]
    gmat = _bdot_nt(cm, bm)
    cross = _bdot(cm, state)
    ys, xws, els = [], [], []
    for p in range(SSD_HPG // 2):
        ha, hb = 2 * p, 2 * p + 1
        lo, hi = p * 2 * SSD_P, (p + 1) * 2 * SSD_P
        xs_p = xs[:, lo:hi]
        xdt = xs_p * jnp.where(first, dt[:, ha:ha + 1], dt[:, hb:hb + 1])
        xdt16 = xdt.astype(BF16)
        dec_a = jnp.exp2(jnp.where(causal, cum[:, ha:ha + 1] - cum_t[ha:ha + 1, :], NEG_BIG))
        dec_b = jnp.exp2(jnp.where(causal, cum[:, hb:hb + 1] - cum_t[hb:hb + 1, :], NEG_BIG))
        ya = jnp.dot((gmat * dec_a).astype(BF16), xdt16, preferred_element_type=F32)
        yb = jnp.dot((gmat * dec_b).astype(BF16), xdt16, preferred_element_type=F32)
        y = jnp.where(first, ya, yb)
        y = y + cross[:, lo:hi] * jnp.where(first, e_cum[:, ha:ha + 1], e_cum[:, hb:hb + 1])
        y = y + dsk_ref[:, lo:hi] * xs_p
        ys.append(y)
        xws.append(xdt * jnp.where(first, e_rem[:, ha:ha + 1], e_rem[:, hb:hb + 1]))
        els.append(jnp.where(first1, e_last[:, ha:ha + 1], e_last[:, hb:hb + 1]))
    y = jnp.concatenate(ys, axis=1)
    xw = jnp.concatenate(xws, axis=1)
    el = jnp.concatenate(els, axis=1)
    state_ref[...] = state * el + _bdot_tn(bm, xw)

    yz = y * _silu(z_ref[...])
    ms = jnp.mean(yz * yz, axis=-1, keepdims=True)
    o_ref[...] = (yz * lax.rsqrt(ms + LN_EPS) * nw_ref[...]).astype(o_ref.dtype)


def _ssd(proj, dt_raw, conv_w, conv_b, dtb, alog, dskip, norm_w):
    b, s, _ = proj.shape
    c = SSD_CHUNK
    gw = SSD_GW
    n = SSD_N
    z0 = (2 * RET_QK + 2 * RET_V) // gw
    x0 = z0 + SSD_DI // gw
    b0 = (2 * RET_QK + 2 * RET_V + SSD_DI + SSD_DI) // n
    c0 = b0 + SSD_G
    wb0 = SSD_DI // n
    wc0 = wb0 + SSD_G
    im3 = lambda f: (lambda bi, g, ci: f(bi, g, ci))
    return pl.pallas_call(
        _ssd_kernel,
        grid=(b, SSD_G, s // c),
        in_specs=[pl.BlockSpec((None, c, gw), im3(lambda bi, g, ci: (bi, ci, z0 + g))),
                  pl.BlockSpec((None, c, gw), im3(lambda bi, g, ci: (bi, ci, x0 + g))),
                  pl.BlockSpec((None, c, n), im3(lambda bi, g, ci: (bi, ci, b0 + g))),
                  pl.BlockSpec((None, c, n), im3(lambda bi, g, ci: (bi, ci, c0 + g))),
                  pl.BlockSpec((None, c, LANES), im3(lambda bi, g, ci: (bi, ci, g))),
                  pl.BlockSpec((SSD_CONV, gw), im3(lambda bi, g, ci: (0, g))),
                  pl.BlockSpec((SSD_CONV, n), im3(lambda bi, g, ci: (0, wb0 + g))),
                  pl.BlockSpec((SSD_CONV, n), im3(lambda bi, g, ci: (0, wc0 + g))),
                  pl.BlockSpec((1, gw), im3(lambda bi, g, ci: (0, g))),
                  pl.BlockSpec((1, n), im3(lambda bi, g, ci: (0, wb0 + g))),
                  pl.BlockSpec((1, n), im3(lambda bi, g, ci: (0, wc0 + g))),
                  pl.BlockSpec((1, LANES), im3(lambda bi, g, ci: (0, g))),
                  pl.BlockSpec((1, LANES), im3(lambda bi, g, ci: (0, g))),
                  pl.BlockSpec((1, gw), im3(lambda bi, g, ci: (0, g))),
                  pl.BlockSpec((1, gw), im3(lambda bi, g, ci: (0, g)))],
        out_specs=pl.BlockSpec((None, c, gw), im3(lambda bi, g, ci: (bi, ci, g))),
        out_shape=jax.ShapeDtypeStruct((b, s, SSD_DI), BF16),
        scratch_shapes=[pltpu.VMEM((n, gw), F32), pltpu.VMEM((c + SUBLANES, gw + 2 * n), F32)],
        compiler_params=_cparams("parallel", "parallel", "arbitrary"),
    )(proj, proj, proj, proj, dt_raw, conv_w, conv_w, conv_w,
      conv_b.reshape(1, SSD_XBC), conv_b.reshape(1, SSD_XBC), conv_b.reshape(1, SSD_XBC),
      dtb, alog, dskip, norm_w.reshape(1, SSD_DI))


def _group_lanes(v):
    v = v.reshape(SSD_G, SSD_HPG)
    return jnp.pad(v, ((0, 0), (0, LANES - SSD_HPG))).reshape(1, SSD_G * LANES)


def _hgrn2_masks():
    c = HG_CHUNK
    t = lax.broadcasted_iota(jnp.int32, (c, c), 0)
    s = lax.broadcasted_iota(jnp.int32, (c, c), 1)
    masks = []
    blk = c
    while blk > SUBLANES:
        half = blk // 2
        m = (t // blk == s // blk) & ((t % blk) >= half) & ((s % blk) < half)
        masks.append((blk, m))
        blk = half
    return masks


def _hgrn2_kernel(q_ref, f_ref, i_ref, g_ref, lbl_ref, nw_ref, o_ref, state_ref, *, layer):
    c = HG_CHUNK
    dk = HG_DK

    @pl.when(pl.program_id(2) == 0)
    def _():
        state_ref[...] = jnp.zeros_like(state_ref)

    lbl = lbl_ref[...]
    ex = jnp.exp(lbl - jnp.max(lbl, axis=0, keepdims=True))
    pr = ex / jnp.sum(ex, axis=0, keepdims=True)
    lb_all = jnp.sum(pr[1:layer + 1, :], axis=0, keepdims=True)
    nw_all = nw_ref[...]

    masks = _hgrn2_masks()
    row = lax.broadcasted_iota(jnp.int32, (c, dk), 0)
    sub = row % SUBLANES
    trow = lax.broadcasted_iota(jnp.int32, (c, c), 0)
    scol = lax.broadcasted_iota(jnp.int32, (c, c), 1)
    tile_base = (trow // SUBLANES) * SUBLANES

    def sub_bcast(x, j):
        x3 = x.reshape(c // SUBLANES, SUBLANES, dk)
        return jnp.broadcast_to(x3[:, j:j + 1, :], x3.shape).reshape(c, dk)

    def head_chunk(r0, hh):
        cols = slice(hh * dk, (hh + 1) * dk)
        lb = lb_all[:, cols]
        fr = f_ref[pl.ds(r0, c), cols]
        sg = _sigmoid(fr)
        log2_f = jnp.log2(lb + (1.0 - lb) * sg)
        kk = (1.0 - lb) * (1.0 - sg)
        qq = _silu(q_ref[pl.ds(r0, c), cols])
        v16 = i_ref[pl.ds(r0, c), cols].astype(BF16)
        cum = _cumsum_rows(log2_f)

        scores = jnp.zeros((c, c), F32)
        for blk, m in masks:
            half = blk // 2
            refv = jnp.concatenate(
                [jnp.broadcast_to(cum[nb * blk + half - 1:nb * blk + half, :], (blk, dk))
                 for nb in range(c // blk)], axis=0)
            e = jnp.exp2(-jnp.abs(cum - refv))
            scores = scores + jnp.where(m, _bdot_nt(qq * e, kk * e), 0.0)
        for j in range(SUBLANES):
            e = jnp.exp2(jnp.where(sub >= j, cum - sub_bcast(cum, j), NEG_BIG))
            col = jnp.sum(qq * e * sub_bcast(kk, j), axis=-1, keepdims=True)
            scores = scores + jnp.where(scol == tile_base + j, col, 0.0)

        state_t = state_ref[hh]
        y = jnp.dot(scores.astype(BF16), v16, preferred_element_type=F32)
        y = y + _bdot_nt(qq * jnp.exp2(cum), state_t)
        last = cum[c - 1:c, :]
        k_dec = kk * jnp.exp2(last - cum)
        state_ref[hh] = state_t * jnp.exp2(last) + _bdot_tn(v16, k_dec)
        ms = jnp.mean(y * y, axis=-1, keepdims=True)
        out = y * lax.rsqrt(ms + LN_EPS) * nw_all[:, cols] * _silu(g_ref[pl.ds(r0, c), cols])
        o_ref[pl.ds(r0, c), cols] = out.astype(o_ref.dtype)

    def chunk(ci, carry):
        r0 = pl.multiple_of(ci * c, c)
        for hh in range(HG_HPS):
            head_chunk(r0, hh)
        return carry

    lax.fori_loop(0, HG_ROWS // c, chunk, 0)


def _hgrn2(proj, lb_logits, norm_w, layer):
    b, s, _ = proj.shape
    r = HG_ROWS
    w = HG_HPS * HG_DK
    nb = HG_HEADS // HG_HPS
    return pl.pallas_call(
        functools.partial(_hgrn2_kernel, layer=layer),
        grid=(b, nb, s // r),
        in_specs=[pl.BlockSpec((None, r, w), lambda bi, hi, ri: (bi, ri, hi)),
                  pl.BlockSpec((None, r, w), lambda bi, hi, ri: (bi, ri, nb + hi)),
                  pl.BlockSpec((None, r, w), lambda bi, hi, ri: (bi, ri, 2 * nb + hi)),
                  pl.BlockSpec((None, r, w), lambda bi, hi, ri: (bi, ri, 3 * nb + hi)),
                  pl.BlockSpec((DEPTH, w), lambda bi, hi, ri: (0, hi)),
                  pl.BlockSpec((1, w), lambda bi, hi, ri: (0, hi))],
        out_specs=pl.BlockSpec((None, r, w), lambda bi, hi, ri: (bi, ri, hi)),
        out_shape=jax.ShapeDtypeStruct((b, s, HG_HEADS * HG_DK), BF16),
        scratch_shapes=[pltpu.VMEM((HG_HPS, HG_DK, HG_DK), F32)],
        compiler_params=_cparams("parallel", "parallel", "arbitrary"),
    )(proj, proj, proj, proj, lb_logits, norm_w.reshape(1, HG_HEADS * HG_DK))


def _router_kernel(x_ref, w_ref, idx_ref, gate_ref, cnt_ref, carry_ref):
    tm = x_ref.shape[0]

    @pl.when(pl.program_id(0) == 0)
    def _():
        carry_ref[...] = jnp.zeros_like(carry_ref)

    logits = jnp.dot(x_ref[...], w_ref[...], preferred_element_type=F32,
                     precision=lax.Precision.HIGHEST)
    lane = lax.broadcasted_iota(jnp.int32, (tm, LANES), 1)
    lane_f = lane.astype(F32)
    lg = jnp.where(lane < N_EXPERTS, logits, NEG_BIG)
    m1 = jnp.max(lg, axis=-1, keepdims=True)
    i1 = jnp.min(jnp.where(lg == m1, lane_f, float(LANES)), axis=-1, keepdims=True).astype(jnp.int32)
    lg2 = jnp.where(lane == i1, NEG_BIG, lg)
    m2 = jnp.max(lg2, axis=-1, keepdims=True)
    i2 = jnp.min(jnp.where(lg2 == m2, lane_f, float(LANES)), axis=-1, keepdims=True).astype(jnp.int32)
    e2 = jnp.exp(m2 - m1)
    den = 1.0 + e2
    gate_ref[...] = jnp.where(lane == 0, 1.0 / den, jnp.where(lane == 1, e2 / den, 0.0))

    sel1 = lane == i1
    sel2 = lane == i2
    assign = (sel1 | sel2).astype(BF16)
    tt = lax.broadcasted_iota(jnp.int32, (tm, tm), 0)
    ss = lax.broadcasted_iota(jnp.int32, (tm, tm), 1)
    before = (ss < tt).astype(BF16)
    base = carry_ref[...] + jnp.dot(before, assign, preferred_element_type=F32)
    r1 = jnp.sum(jnp.where(sel1, base, 0.0), axis=-1, keepdims=True).astype(jnp.int32)
    r2 = jnp.sum(jnp.where(sel2, base, 0.0), axis=-1, keepdims=True).astype(jnp.int32)
    idx_ref[...] = jnp.where(lane == 0, i1, jnp.where(lane == 1, i2,
                             jnp.where(lane == 2, r1, jnp.where(lane == 3, r2, 0))))
    carry_ref[...] += jnp.sum(assign.astype(F32), axis=0, keepdims=True)
    cnt_ref[...] = carry_ref[...]


def _router(x32, w_router, tm=512):
    n, d = x32.shape
    wr = jnp.pad(w_router, ((0, 0), (0, LANES - N_EXPERTS)))
    return pl.pallas_call(
        _router_kernel,
        grid=(n // tm,),
        in_specs=[pl.BlockSpec((tm, d), lambda i: (i, 0)),
                  pl.BlockSpec((d, LANES), lambda i: (0, 0))],
        out_specs=[pl.BlockSpec((tm, LANES), lambda i: (i, 0)),
                   pl.BlockSpec((tm, LANES), lambda i: (i, 0)),
                   pl.BlockSpec((1, LANES), lambda i: (0, 0))],
        out_shape=[jax.ShapeDtypeStruct((n, LANES), jnp.int32),
                   jax.ShapeDtypeStruct((n, LANES), F32),
                   jax.ShapeDtypeStruct((1, LANES), F32)],
        scratch_shapes=[pltpu.VMEM((1, LANES), F32)],
        compiler_params=_cparams("arbitrary"),
    )(x32, wr)


def _combine_kernel(pos_ref, y_hbm, gate_ref, x_ref, g_ref, b_ref, o32_ref, o16_ref, buf_ref, sem):
    i = pl.program_id(0)
    nblk = pl.num_programs(0)
    tc = x_ref.shape[0]
    slot = i % 2

    def gather(blk, slt, wait):
        def body(r, carry):
            for k in range(2):
                cp = _row_copy(y_hbm, buf_ref.at[slt, k], pos_ref[2 * (blk * tc + r) + k], r, sem.at[slt])
                if wait:
                    cp.wait()
                else:
                    cp.start()
            return carry
        lax.fori_loop(0, tc, body, 0, unroll=2)

    @pl.when(i == 0)
    def _():
        gather(0, 0, wait=False)

    @pl.when(i + 1 < nblk)
    def _():
        gather(i + 1, 1 - slot, wait=False)

    gather(i, slot, wait=True)
    gates = gate_ref[...]
    ffn = gates[:, 0:1] * buf_ref[slot, 0] + gates[:, 1:2] * buf_ref[slot, 1]
    out = _layer_norm_rows(ALPHA * x_ref[...] + ffn, g_ref[...], b_ref[...])
    o32_ref[...] = out
    o16_ref[...] = out.astype(BF16)


def _combine(pos, yr, gates, x32, ln_g, ln_b, tc=GATHER_ROWS):
    n, d = x32.shape
    return pl.pallas_call(
        _combine_kernel,
        grid_spec=pltpu.PrefetchScalarGridSpec(
            num_scalar_prefetch=1, grid=(n // tc,),
            in_specs=[pl.BlockSpec(memory_space=pl.ANY),
                      pl.BlockSpec((tc, LANES), lambda i, p: (i, 0)),
                      pl.BlockSpec((tc, d), lambda i, p: (i, 0)),
                      pl.BlockSpec((1, d), lambda i, p: (0, 0)),
                      pl.BlockSpec((1, d), lambda i, p: (0, 0))],
            out_specs=[pl.BlockSpec((tc, d), lambda i, p: (i, 0)),
                       pl.BlockSpec((tc, d), lambda i, p: (i, 0))],
            scratch_shapes=[pltpu.VMEM((2, 2, tc, d), F32), pltpu.SemaphoreType.DMA((2,))]),
        out_shape=[jax.ShapeDtypeStruct((n, d), F32), jax.ShapeDtypeStruct((n, d), BF16)],
        compiler_params=_cparams("arbitrary"),
    )(pos, yr, gates, x32, ln_g.reshape(1, d), ln_b.reshape(1, d))


def _moe(x32, w_router, w1, w3, w2, layer, ln_g, ln_b):
    n, d = x32.shape
    tm = MOE_TM
    idx, gates, counts = _router(x32, w_router)
    counts = counts[0, :N_EXPERTS].astype(jnp.int32)
    expert = idx[:, 0:2]
    rank = idx[:, 2:4]
    padded = (counts + tm - 1) // tm * tm
    padded_end = jnp.cumsum(padded)
    start_padded = padded_end - padded
    dest = start_padded[expert] + rank
    n_rows = 2 * n + N_EXPERTS * tm
    n_blocks = n_rows // tm
    token = jnp.broadcast_to(jnp.arange(n, dtype=jnp.int32)[:, None], (n, 2))
    row_token = jnp.zeros((n_rows,), jnp.int32).at[dest.reshape(-1)].set(token.reshape(-1))
    block_start = jnp.arange(n_blocks, dtype=jnp.int32) * tm
    blk_valid = (block_start < padded_end[-1]).astype(jnp.int32)
    last_expert = jnp.minimum(jnp.searchsorted(padded_end, padded_end[-1] - 1, side="right"), N_EXPERTS - 1)
    blk_expert = jnp.minimum(jnp.searchsorted(padded_end, block_start, side="right"), N_EXPERTS - 1)
    blk_expert = jnp.where(blk_valid > 0, blk_expert, last_expert).astype(jnp.int32)
    yr = _moe_ffn(x32, row_token, w1, w3, w2, layer, blk_expert, blk_valid, tm)
    return _combine(dest.reshape(-1).astype(jnp.int32), yr, gates, x32, ln_g, ln_b)


def _rope_tables(s):
    inv_freq = ROPE_BASE ** (-jnp.arange(0, RET_DK, 2, dtype=F32) / RET_DK)
    ang = jnp.arange(s, dtype=F32)[:, None] * inv_freq[None, :]
    cos = jnp.cos(ang)
    sin = jnp.sin(ang)
    return jnp.concatenate([cos, cos], axis=-1), jnp.concatenate([-sin, sin], axis=-1)


def _even_layer(x32, x16, b, s, j, w_in, ret_norm_w, conv_w, conv_b, dt_bias, a_log, d_skip,
                ssd_norm_w, w_out, ln1_g, ln1_b, w1, w3, w2, ln2_g, ln2_b):
    n = b * s
    w_dt = w_in[j, :, EV_MAIN:].reshape(D_MODEL, SSD_G, SSD_HPG)
    w_dt = jnp.pad(w_dt, ((0, 0), (0, 0), (0, LANES - SSD_HPG))).reshape(1, D_MODEL, SSD_G * LANES)
    proj = _matmul(x16, w_in, j, 1024, 1024, F32, n=EV_MAIN).reshape(b, s, EV_MAIN)
    dt_raw = _matmul(x16, w_dt, 0, 1024, SSD_G * LANES, F32).reshape(b, s, SSD_G * LANES)
    log_gamma = jnp.log1p(-jnp.exp2(-5.0 - jnp.arange(RET_HEADS, dtype=F32)))
    cos_t, sin_t = _rope_tables(s)
    ret = _retention(proj, log_gamma, cos_t, sin_t, ret_norm_w)
    ssd = _ssd(proj, dt_raw, conv_w, conv_b, _group_lanes(dt_bias), _group_lanes(a_log),
               jnp.repeat(d_skip, SSD_P).reshape(1, SSD_DI), ssd_norm_w)
    x32, x16 = _matmul_ln([ret.reshape(n, RET_V), ssd.reshape(n, SSD_DI)], w_out, j, x32, ln1_g, ln1_b)
    return _ffn_ln(x16, w1, w3, w2, j, x32, ln2_g, ln2_b)


def _odd_layer(x32, x16, b, s, layer, w_in, lb_logits, hg_norm_w, w_out, ln1_g, ln1_b,
               w_router, w1, w3, w2, ln2_g, ln2_b):
    n = b * s
    j = layer // 2
    proj = _matmul(x16, w_in, j, 1024, 1024, F32).reshape(b, s, -1)
    o = _hgrn2(proj, lb_logits, hg_norm_w, layer)
    x32, x16 = _matmul_ln([o.reshape(n, -1)], w_out, j, x32, ln1_g, ln1_b)
    return _moe(x32, w_router, w1, w3, w2, j, ln2_g, ln2_b)


def kernel(x, ev_w_in, ev_ret_norm_w, ev_conv_w, ev_conv_b, ev_dt_bias, ev_a_log, ev_d_skip,
           ev_ssd_norm_w, ev_w_out, ev_ln1_g, ev_ln1_b, ffn_w1, ffn_w3, ffn_w2, ev_ln2_g, ev_ln2_b,
           od_w_in, hg_lb_logits, od_hg_norm_w, od_w_out, od_ln1_g, od_ln1_b, moe_router,
           moe_w1, moe_w3, moe_w2, od_ln2_g, od_ln2_b):
    b, s, d = x.shape
    x32 = x.reshape(b * s, d)
    x16 = x32.astype(BF16)
    ev_w_in, ev_w_out, ffn_w1, ffn_w3, ffn_w2, od_w_in, od_w_out, moe_w1, moe_w3, moe_w2 = (
        w.astype(BF16) for w in (ev_w_in, ev_w_out, ffn_w1, ffn_w3, ffn_w2, od_w_in, od_w_out,
                                 moe_w1, moe_w3, moe_w2))
    for layer in range(DEPTH):
        j = layer // 2
        if layer % 2 == 0:
            x32, x16 = _even_layer(x32, x16, b, s, j, ev_w_in, ev_ret_norm_w[j], ev_conv_w[j], ev_conv_b[j],
                                   ev_dt_bias[j], ev_a_log[j], ev_d_skip[j], ev_ssd_norm_w[j], ev_w_out,
                                   ev_ln1_g[j], ev_ln1_b[j], ffn_w1, ffn_w3, ffn_w2,
                                   ev_ln2_g[j], ev_ln2_b[j])
        else:
            x32, x16 = _odd_layer(x32, x16, b, s, layer, od_w_in, hg_lb_logits, od_hg_norm_w[j],
                                  od_w_out, od_ln1_g[j], od_ln1_b[j], moe_router[j],
                                  moe_w1, moe_w3, moe_w2, od_ln2_g[j], od_ln2_b[j])
    return x32.reshape(b, s, d)
```

```python
import functools
import math

import numpy as np
import jax
import jax.numpy as jnp
from jax import lax
from jax.experimental import pallas as pl
from jax.experimental.pallas import tpu as pltpu

F32 = jnp.float32
BF16 = jnp.bfloat16

D_MODEL = 2048
DEPTH = 4
LN_EPS = 1e-5
RET_HEADS = 8
RET_DK = 128
RET_DV = 256
RET_QK = RET_HEADS * RET_DK
RET_V = RET_HEADS * RET_DV
ROPE_BASE = 10000.0
SSD_DI = 2048
SSD_P = 64
SSD_HEADS = 32
SSD_G = 4
SSD_N = 128
SSD_CONV = 4
SSD_XBC = SSD_DI + 2 * SSD_G * SSD_N
SSD_HPG = SSD_HEADS // SSD_G
SSD_GW = SSD_HPG * SSD_P
HG_DK = 128
HG_HEADS = 16
FFN_DIM = 5632
N_EXPERTS = 8
EV_MAIN = 2 * RET_QK + 2 * RET_V + SSD_DI + SSD_XBC
ALPHA = (2.0 * DEPTH) ** 0.25

LANES = 128
SUBLANES = 8
VMEM_LIMIT = 52 * 1024 * 1024
NEG_BIG = -1e30
LOG2_E = math.log2(math.e)

RET_CHUNK = 256
RET_HPS = 4
SSD_CHUNK = 256
HG_CHUNK = 64
HG_ROWS = 512
HG_HPS = 16
MOE_TM = 512
GATHER_ROWS = 256
FFN_SPLIT = 2


def _cparams(*sem):
    return pltpu.CompilerParams(dimension_semantics=sem, vmem_limit_bytes=VMEM_LIMIT)


def _sigmoid(x):
    return 0.5 * jnp.tanh(0.5 * x) + 0.5


def _silu(x):
    return x * _sigmoid(x)


def _bdot(a, b):
    return jnp.dot(a.astype(BF16), b.astype(BF16), preferred_element_type=F32)


def _bdot_nt(a, b):
    return lax.dot_general(a.astype(BF16), b.astype(BF16), (((1,), (1,)), ((), ())),
                           preferred_element_type=F32)


def _bdot_tn(a, b):
    return lax.dot_general(a.astype(BF16), b.astype(BF16), (((0,), (0,)), ((), ())),
                           preferred_element_type=F32)


def _layer_norm_rows(y, g, b):
    mu = jnp.mean(y, axis=-1, keepdims=True)
    yc = y - mu
    var = jnp.mean(yc * yc, axis=-1, keepdims=True)
    return yc * lax.rsqrt(var + LN_EPS) * g + b


def _cumsum_rows(x):
    n = x.shape[0]
    row = lax.broadcasted_iota(jnp.int32, x.shape, 0)
    shift = 1
    while shift < n:
        x = x + jnp.where(row >= shift, pltpu.roll(x, shift, 0), 0.0)
        shift *= 2
    return x


def _mm_kernel(x_ref, w_ref, o_ref):
    o_ref[...] = jnp.dot(x_ref[...], w_ref[...], preferred_element_type=F32).astype(o_ref.dtype)


def _matmul(x, w, layer, tm, tn, out_dtype, n=None):
    m, k = x.shape
    n = w.shape[2] if n is None else n
    return pl.pallas_call(
        _mm_kernel,
        grid=(m // tm, n // tn),
        in_specs=[pl.BlockSpec((tm, k), lambda i, j: (i, 0)),
                  pl.BlockSpec((None, k, tn), lambda i, j: (layer, 0, j))],
        out_specs=pl.BlockSpec((tm, tn), lambda i, j: (i, j)),
        out_shape=jax.ShapeDtypeStruct((m, n), out_dtype),
        compiler_params=_cparams("parallel", "parallel"),
    )(x, w)


def _mm_ln_kernel(*refs, n_in, kpi):
    a_refs = refs[:n_in]
    w_ref, r_ref, g_ref, b_ref, o32_ref, o16_ref, acc_ref = refs[n_in:]
    k = pl.program_id(1)

    @pl.when(k == 0)
    def _():
        acc_ref[...] = jnp.zeros_like(acc_ref)

    def accumulate(a_ref):
        rows = a_ref.shape[0] // FFN_SPLIT
        for r in range(FFN_SPLIT):
            sl = slice(r * rows, (r + 1) * rows)
            acc_ref[sl, :] += jnp.dot(a_ref[sl, :], w_ref[...], preferred_element_type=F32)

    if n_in == 1:
        accumulate(a_refs[0])
    else:
        for j in range(n_in):
            @pl.when((k >= j * kpi) & (k < (j + 1) * kpi))
            def _(j=j):
                accumulate(a_refs[j])

    @pl.when(k == n_in * kpi - 1)
    def _():
        y = ALPHA * r_ref[...] + acc_ref[...]
        out = _layer_norm_rows(y, g_ref[...], b_ref[...])
        o32_ref[...] = out
        o16_ref[...] = out.astype(BF16)


def _matmul_ln(a_list, w, layer, resid, ln_g, ln_b, tm=512, tk=1024):
    n_in = len(a_list)
    m, ka = a_list[0].shape
    kpi = ka // tk
    n = w.shape[2]

    def a_map(j):
        return lambda i, k: (i, jnp.clip(k - j * kpi, 0, kpi - 1))

    in_specs = [pl.BlockSpec((tm, tk), a_map(j)) for j in range(n_in)]
    in_specs += [pl.BlockSpec((None, tk, n), lambda i, k: (layer, k, 0)),
                 pl.BlockSpec((tm, n), lambda i, k: (i, 0)),
                 pl.BlockSpec((1, n), lambda i, k: (0, 0)),
                 pl.BlockSpec((1, n), lambda i, k: (0, 0))]
    return pl.pallas_call(
        functools.partial(_mm_ln_kernel, n_in=n_in, kpi=kpi),
        grid=(m // tm, n_in * kpi),
        in_specs=in_specs,
        out_specs=[pl.BlockSpec((tm, n), lambda i, k: (i, 0)),
                   pl.BlockSpec((tm, n), lambda i, k: (i, 0))],
        out_shape=[jax.ShapeDtypeStruct((m, n), F32), jax.ShapeDtypeStruct((m, n), BF16)],
        scratch_shapes=[pltpu.VMEM((tm, n), F32)],
        compiler_params=_cparams("parallel", "arbitrary"),
    )(*a_list, w, resid, ln_g.reshape(1, n), ln_b.reshape(1, n))


def _swiglu_accumulate(x_ref, w1_ref, w3_ref, w2_ref, o32_ref, j):
    @pl.when(j == 0)
    def _():
        o32_ref[...] = jnp.zeros_like(o32_ref)

    rows = x_ref.shape[0] // FFN_SPLIT
    for r in range(FFN_SPLIT):
        sl = slice(r * rows, (r + 1) * rows)
        x16 = x_ref[sl, :]
        h1 = jnp.dot(x16, w1_ref[...], preferred_element_type=F32)
        h3 = jnp.dot(x16, w3_ref[...], preferred_element_type=F32)
        h = (_silu(h1) * h3).astype(BF16)
        o32_ref[sl, :] += jnp.dot(h, w2_ref[...], preferred_element_type=F32)


def _ffn_ln_kernel(x_ref, w1_ref, w3_ref, w2_ref, r_ref, g_ref, b_ref, o32_ref, o16_ref):
    j = pl.program_id(1)
    _swiglu_accumulate(x_ref, w1_ref, w3_ref, w2_ref, o32_ref, j)

    @pl.when(j == pl.num_programs(1) - 1)
    def _():
        y = ALPHA * r_ref[...] + o32_ref[...]
        out = _layer_norm_rows(y, g_ref[...], b_ref[...])
        o32_ref[...] = out
        o16_ref[...] = out.astype(BF16)


def _ffn_ln(x16, w1, w3, w2, layer, resid, ln_g, ln_b, tm=512, tf=512):
    rows, d = x16.shape
    f = w1.shape[2]
    row_blk = pl.BlockSpec((tm, d), lambda i, j: (i, 0))
    vec = pl.BlockSpec((1, d), lambda i, j: (0, 0))
    return pl.pallas_call(
        _ffn_ln_kernel,
        grid=(rows // tm, f // tf),
        in_specs=[row_blk,
                  pl.BlockSpec((None, d, tf), lambda i, j: (layer, 0, j)),
                  pl.BlockSpec((None, d, tf), lambda i, j: (layer, 0, j)),
                  pl.BlockSpec((None, tf, d), lambda i, j: (layer, j, 0)),
                  row_blk, vec, vec],
        out_specs=[row_blk, row_blk],
        out_shape=[jax.ShapeDtypeStruct((rows, d), F32), jax.ShapeDtypeStruct((rows, d), BF16)],
        compiler_params=_cparams("parallel", "arbitrary"),
    )(x16, w1, w3, w2, resid, ln_g.reshape(1, d), ln_b.reshape(1, d))


def _row_copy(src_hbm, dst_vmem, src_row, dst_row, sem):
    return pltpu.make_async_copy(src_hbm.at[pl.ds(src_row, 1), :], dst_vmem.at[pl.ds(dst_row, 1), :], sem)


def _moe_ffn_kernel(be_ref, bv_ref, tok_ref, x_hbm, w1_ref, w3_ref, w2_ref, o32_ref, buf_ref, x16_ref, sem):
    i = pl.program_id(0)
    j = pl.program_id(1)
    nblk = pl.num_programs(0)
    tm = o32_ref.shape[0]
    valid = bv_ref[i] > 0
    slot = i % 2

    def gather(blk, slt, wait):
        if wait:
            pltpu.make_async_copy(x_hbm.at[pl.ds(0, tm), :], buf_ref.at[slt], sem.at[slt]).wait()
            return

        def body(r, carry):
            _row_copy(x_hbm, buf_ref.at[slt], tok_ref[blk * tm + r], r, sem.at[slt]).start()
            return carry
        lax.fori_loop(0, tm, body, 0, unroll=4)

    @pl.when((j == 0) & (i == 0) & valid)
    def _():
        gather(0, 0, wait=False)

    @pl.when((j == 0) & (i + 1 < nblk))
    def _():
        @pl.when(bv_ref[i + 1] > 0)
        def _():
            gather(i + 1, 1 - slot, wait=False)

    @pl.when((j == 0) & valid)
    def _():
        gather(i, slot, wait=True)
        x16_ref[...] = buf_ref[slot].astype(BF16)

    @pl.when(valid)
    def _():
        _swiglu_accumulate(x16_ref, w1_ref, w3_ref, w2_ref, o32_ref, j)

    @pl.when(jnp.logical_not(valid) & (j == 0))
    def _():
        o32_ref[...] = jnp.zeros_like(o32_ref)


def _moe_ffn(x32, row_token, w1, w3, w2, layer, blk_expert, blk_valid, tm, tf=512):
    n_rows = row_token.shape[0]
    d = x32.shape[1]
    nj = w1.shape[3] // tf

    def jeff(i, j, bv):
        return jnp.where(bv[i] > 0, j, nj - 1)

    return pl.pallas_call(
        _moe_ffn_kernel,
        grid_spec=pltpu.PrefetchScalarGridSpec(
            num_scalar_prefetch=3, grid=(n_rows // tm, nj),
            in_specs=[pl.BlockSpec(memory_space=pl.ANY),
                      pl.BlockSpec((None, None, d, tf), lambda i, j, be, bv, tok: (layer, be[i], 0, jeff(i, j, bv))),
                      pl.BlockSpec((None, None, d, tf), lambda i, j, be, bv, tok: (layer, be[i], 0, jeff(i, j, bv))),
                      pl.BlockSpec((None, None, tf, d), lambda i, j, be, bv, tok: (layer, be[i], jeff(i, j, bv), 0))],
            out_specs=pl.BlockSpec((tm, d), lambda i, j, be, bv, tok: (i, 0)),
            scratch_shapes=[pltpu.VMEM((2, tm, d), F32), pltpu.VMEM((tm, d), BF16),
                            pltpu.SemaphoreType.DMA((2,))]),
        out_shape=jax.ShapeDtypeStruct((n_rows, d), F32),
        compiler_params=_cparams("arbitrary", "arbitrary"),
    )(blk_expert, blk_valid, row_token, x32, w1, w3, w2)


def _retention_kernel(lg_ref, q_ref, k_ref, v_ref, g_ref, cos_ref, sin_ref, nw_ref, o_ref, state_ref):
    c = RET_CHUNK
    hp = pl.program_id(1)

    @pl.when(pl.program_id(2) == 0)
    def _():
        state_ref[...] = jnp.zeros_like(state_ref)

    cos = cos_ref[...]
    sin = sin_ref[...]
    t = lax.broadcasted_iota(jnp.int32, (c, 1), 0).astype(F32)
    tt = lax.broadcasted_iota(jnp.int32, (c, c), 0)
    ss = lax.broadcasted_iota(jnp.int32, (c, c), 1)
    causal = tt >= ss
    lag = (tt - ss).astype(F32)
    for hh in range(RET_HPS):
        qk = slice(hh * RET_DK, (hh + 1) * RET_DK)
        vv = slice(hh * RET_DV, (hh + 1) * RET_DV)
        lg = lg_ref[hp * RET_HPS + hh] * LOG2_E
        q = q_ref[:, qk]
        k = k_ref[:, qk]
        qr = q * cos + pltpu.roll(q, RET_DK // 2, 1) * sin
        kr = (k * cos + pltpu.roll(k, RET_DK // 2, 1) * sin) * (RET_DK ** -0.5)
        decay = jnp.exp2(jnp.where(causal, lag * lg, NEG_BIG))
        v = v_ref[:, vv].astype(BF16)
        state = state_ref[hh]
        scores = _bdot_nt(qr, kr) * decay
        y = jnp.dot(scores.astype(BF16), v, preferred_element_type=F32)
        y = y + _bdot(qr * jnp.exp2((t + 1.0) * lg), state)
        k_dec = kr * jnp.exp2((float(c - 1) - t) * lg)
        state_ref[hh] = jnp.exp2(jnp.full((1, 1), float(c), F32) * lg) * state + _bdot_tn(k_dec, v)
        mu = jnp.mean(y, axis=-1, keepdims=True)
        yc = y - mu
        var = jnp.mean(yc * yc, axis=-1, keepdims=True)
        yn = yc * lax.rsqrt(var + LN_EPS) * nw_ref[:, vv]
        o_ref[:, vv] = (yn * _silu(g_ref[:, vv])).astype(o_ref.dtype)


def _retention(proj, log_gamma, cos_t, sin_t, norm_w):
    b, s, _ = proj.shape
    c = RET_CHUNK
    qw = RET_HPS * RET_DK
    vw = RET_HPS * RET_DV
    kb = RET_QK // qw
    vb = 2 * RET_QK // vw
    gb = vb + RET_V // vw
    return pl.pallas_call(
        _retention_kernel,
        grid_spec=pltpu.PrefetchScalarGridSpec(
            num_scalar_prefetch=1, grid=(b, RET_HEADS // RET_HPS, s // c),
            in_specs=[pl.BlockSpec((None, c, qw), lambda bi, h, ci, lg: (bi, ci, h)),
                      pl.BlockSpec((None, c, qw), lambda bi, h, ci, lg: (bi, ci, kb + h)),
                      pl.BlockSpec((None, c, vw), lambda bi, h, ci, lg: (bi, ci, vb + h)),
                      pl.BlockSpec((None, c, vw), lambda bi, h, ci, lg: (bi, ci, gb + h)),
                      pl.BlockSpec((c, RET_DK), lambda bi, h, ci, lg: (ci, 0)),
                      pl.BlockSpec((c, RET_DK), lambda bi, h, ci, lg: (ci, 0)),
                      pl.BlockSpec((1, vw), lambda bi, h, ci, lg: (0, h))],
            out_specs=pl.BlockSpec((None, c, vw), lambda bi, h, ci, lg: (bi, ci, h)),
            scratch_shapes=[pltpu.VMEM((RET_HPS, RET_DK, RET_DV), F32)]),
        out_shape=jax.ShapeDtypeStruct((b, s, RET_V), BF16),
        compiler_params=_cparams("parallel", "parallel", "arbitrary"),
    )(log_gamma, proj, proj, proj, proj, cos_t, sin_t, norm_w.reshape(1, RET_V))


def _softplus(x):
    return jnp.maximum(x, 0.0) + jnp.log1p(jnp.exp(-jnp.abs(x)))


def _ssd_kernel(z_ref, xs_ref, bm_ref, cm_ref, dt_ref, cwx_ref, cwb_ref, cwc_ref,
                cbx_ref, cbb_ref, cbc_ref, dtb_ref, alog_ref, dsk_ref, nw_ref,
                o_ref, state_ref, ext_ref):
    c = SSD_CHUNK
    gw = SSD_GW
    n = SSD_N
    pad = SUBLANES

    @pl.when(pl.program_id(2) == 0)
    def _():
        state_ref[...] = jnp.zeros_like(state_ref)
        ext_ref[0:pad, :] = jnp.zeros((pad, gw + 2 * n), F32)

    ext_ref[pad:pad + c, 0:gw] = xs_ref[...]
    ext_ref[pad:pad + c, gw:gw + n] = bm_ref[...]
    ext_ref[pad:pad + c, gw + n:gw + 2 * n] = cm_ref[...]
    cw = jnp.concatenate([cwx_ref[...], cwb_ref[...], cwc_ref[...]], axis=1)
    cb = jnp.concatenate([cbx_ref[...], cbb_ref[...], cbc_ref[...]], axis=1)
    conv = cb + cw[SSD_CONV - 1:SSD_CONV, :] * ext_ref[pad:pad + c, :]
    for j in range(SSD_CONV - 1):
        off = pad - (SSD_CONV - 1) + j
        conv = conv + cw[j:j + 1, :] * ext_ref[off:off + c, :]
    tail = ext_ref[c:c + pad, :]
    ext_ref[0:pad, :] = tail
    conv = _silu(conv)
    xs = conv[:, 0:gw]
    bm = conv[:, gw:gw + n]
    cm = conv[:, gw + n:gw + 2 * n]

    dt = _softplus(dt_ref[...] + dtb_ref[...])
    log2_a = -dt * (jnp.exp(alog_ref[...]) * LOG2_E)
    cum = _cumsum_rows(log2_a)
    cum_t = cum.T
    last = cum[c - 1:c, :]
    e_cum = jnp.exp2(cum)
    e_rem = jnp.exp2(last - cum)
    e_last = jnp.exp2(last)

    tt = lax.broadcasted_iota(jnp.int32, (c, c), 0)
    ss = lax.broadcasted_iota(jnp.int32, (c, c), 1)
    causal = tt >= ss
    lane = lax.broadcasted_iota(jnp.int32, (c, 2 * SSD_P), 1)
    first = lane < SSD_P
    lane1 = lax.broadcasted_iota(jnp.int32, (1, 2 * SSD_P), 1)
    first1 = lane1 < SSD_P

    state = state_ref[...]
    gmat = _bdot_nt(cm, bm)
    cross = _bdot(cm, state)
    ys, xws, els = [], [], []
    for p in range(SSD_HPG // 2):
        ha, hb = 2 * p, 2 * p + 1
        lo, hi = p * 2 * SSD_P, (p + 1) * 2 * SSD_P
        xs_p = xs[:, lo:hi]
        xdt = xs_p * jnp.where(first, dt[:, ha:ha + 1], dt[:, hb:hb + 1])
        xdt16 = xdt.astype(BF16)
        dec_a = jnp.exp2(jnp.where(causal, cum[:, ha:ha + 1] - cum_t[ha:ha + 1, :], NEG_BIG))
        dec_b = jnp.exp2(jnp.where(causal, cum[:, hb:hb + 1] - cum_t[hb:hb + 1, :], NEG_BIG))
        ya = jnp.dot((gmat * dec_a).astype(BF16), xdt16, preferred_element_type=F32)
        yb = jnp.dot((gmat * dec_b).astype(BF16), xdt16, preferred_element_type=F32)
        y = jnp.where(first, ya, yb)
        y = y + cross[:, lo:hi] * jnp.where(first, e_cum[:, ha:ha + 1], e_cum[:, hb:hb + 1])
        y = y + dsk_ref[:, lo:hi] * xs_p
        ys.append(y)
        xws.append(xdt * jnp.where(first, e_rem[:, ha:ha + 1], e_rem[:, hb:hb + 1]))
        els.append(jnp.where(first1, e_last[:, ha:ha + 1], e_last[:, hb:hb + 1]))
    y = jnp.concatenate(ys, axis=1)
    xw = jnp.concatenate(xws, axis=1)
    el = jnp.concatenate(els, axis=1)
    state_ref[...] = state * el + _bdot_tn(bm, xw)

    yz = y * _silu(z_ref[...])
    ms = jnp.mean(yz * yz, axis=-1, keepdims=True)
    o_ref[...] = (yz * lax.rsqrt(ms + LN_EPS) * nw_ref[...]).astype(o_ref.dtype)


def _ssd(proj, dt_raw, conv_w, conv_b, dtb, alog, dskip, norm_w):
    b, s, _ = proj.shape
    c = SSD_CHUNK
    gw = SSD_GW
    n = SSD_N
    z0 = (2 * RET_QK + 2 * RET_V) // gw
    x0 = z0 + SSD_DI // gw
    b0 = (2 * RET_QK + 2 * RET_V + SSD_DI + SSD_DI) // n
    c0 = b0 + SSD_G
    wb0 = SSD_DI // n
    wc0 = wb0 + SSD_G
    im3 = lambda f: (lambda bi, g, ci: f(bi, g, ci))
    return pl.pallas_call(
        _ssd_kernel,
        grid=(b, SSD_G, s // c),
        in_specs=[pl.BlockSpec((None, c, gw), im3(lambda bi, g, ci: (bi, ci, z0 + g))),
                  pl.BlockSpec((None, c, gw), im3(lambda bi, g, ci: (bi, ci, x0 + g))),
                  pl.BlockSpec((None, c, n), im3(lambda bi, g, ci: (bi, ci, b0 + g))),
                  pl.BlockSpec((None, c, n), im3(lambda bi, g, ci: (bi, ci, c0 + g))),
                  pl.BlockSpec((None, c, LANES), im3(lambda bi, g, ci: (bi, ci, g))),
                  pl.BlockSpec((SSD_CONV, gw), im3(lambda bi, g, ci: (0, g))),
                  pl.BlockSpec((SSD_CONV, n), im3(lambda bi, g, ci: (0, wb0 + g))),
                  pl.BlockSpec((SSD_CONV, n), im3(lambda bi, g, ci: (0, wc0 + g))),
                  pl.BlockSpec((1, gw), im3(lambda bi, g, ci: (0, g))),
                  pl.BlockSpec((1, n), im3(lambda bi, g, ci: (0, wb0 + g))),
                  pl.BlockSpec((1, n), im3(lambda bi, g, ci: (0, wc0 + g))),
                  pl.BlockSpec((1, LANES), im3(lambda bi, g, ci: (0, g))),
                  pl.BlockSpec((1, LANES), im3(lambda bi, g, ci: (0, g))),
                  pl.BlockSpec((1, gw), im3(lambda bi, g, ci: (0, g))),
                  pl.BlockSpec((1, gw), im3(lambda bi, g, ci: (0, g)))],
        out_specs=pl.BlockSpec((None, c, gw), im3(lambda bi, g, ci: (bi, ci, g))),
        out_shape=jax.ShapeDtypeStruct((b, s, SSD_DI), BF16),
        scratch_shapes=[pltpu.VMEM((n, gw), F32), pltpu.VMEM((c + SUBLANES, gw + 2 * n), F32)],
        compiler_params=_cparams("parallel", "parallel", "arbitrary"),
    )(proj, proj, proj, proj, dt_raw, conv_w, conv_w, conv_w,
      conv_b.reshape(1, SSD_XBC), conv_b.reshape(1, SSD_XBC), conv_b.reshape(1, SSD_XBC),
      dtb, alog, dskip, norm_w.reshape(1, SSD_DI))


def _group_lanes(v):
    v = v.reshape(SSD_G, SSD_HPG)
    return jnp.pad(v, ((0, 0), (0, LANES - SSD_HPG))).reshape(1, SSD_G * LANES)


def _hgrn2_masks():
    c = HG_CHUNK
    t = lax.broadcasted_iota(jnp.int32, (c, c), 0)
    s = lax.broadcasted_iota(jnp.int32, (c, c), 1)
    masks = []
    blk = c
    while blk > SUBLANES:
        half = blk // 2
        m = (t // blk == s // blk) & ((t % blk) >= half) & ((s % blk) < half)
        masks.append((blk, m))
        blk = half
    return masks


def _hgrn2_kernel(q_ref, f_ref, i_ref, g_ref, lbl_ref, nw_ref, o_ref, state_ref, *, layer):
    c = HG_CHUNK
    dk = HG_DK

    @pl.when(pl.program_id(2) == 0)
    def _():
        state_ref[...] = jnp.zeros_like(state_ref)

    lbl = lbl_ref[...]
    ex = jnp.exp(lbl - jnp.max(lbl, axis=0, keepdims=True))
    pr = ex / jnp.sum(ex, axis=0, keepdims=True)
    lb_all = jnp.sum(pr[1:layer + 1, :], axis=0, keepdims=True)
    nw_all = nw_ref[...]

    masks = _hgrn2_masks()
    row = lax.broadcasted_iota(jnp.int32, (c, dk), 0)
    sub = row % SUBLANES
    trow = lax.broadcasted_iota(jnp.int32, (c, c), 0)
    scol = lax.broadcasted_iota(jnp.int32, (c, c), 1)
    tile_base = (trow // SUBLANES) * SUBLANES

    def sub_bcast(x, j):
        x3 = x.reshape(c // SUBLANES, SUBLANES, dk)
        return jnp.broadcast_to(x3[:, j:j + 1, :], x3.shape).reshape(c, dk)

    def head_chunk(r0, hh):
        cols = slice(hh * dk, (hh + 1) * dk)
        lb = lb_all[:, cols]
        fr = f_ref[pl.ds(r0, c), cols]
        sg = _sigmoid(fr)
        log2_f = jnp.log2(lb + (1.0 - lb) * sg)
        kk = (1.0 - lb) * (1.0 - sg)
        qq = _silu(q_ref[pl.ds(r0, c), cols])
        v16 = i_ref[pl.ds(r0, c), cols].astype(BF16)
        cum = _cumsum_rows(log2_f)

        scores = jnp.zeros((c, c), F32)
        for blk, m in masks:
            half = blk // 2
            refv = jnp.concatenate(
                [jnp.broadcast_to(cum[nb * blk + half - 1:nb * blk + half, :], (blk, dk))
                 for nb in range(c // blk)], axis=0)
            e = jnp.exp2(-jnp.abs(cum - refv))
            scores = scores + jnp.where(m, _bdot_nt(qq * e, kk * e), 0.0)
        for j in range(SUBLANES):
            e = jnp.exp2(jnp.where(sub >= j, cum - sub_bcast(cum, j), NEG_BIG))
            col = jnp.sum(qq * e * sub_bcast(kk, j), axis=-1, keepdims=True)
            scores = scores + jnp.where(scol == tile_base + j, col, 0.0)

        state_t = state_ref[hh]
        y = jnp.dot(scores.astype(BF16), v16, preferred_element_type=F32)
        y = y + _bdot_nt(qq * jnp.exp2(cum), state_t)
        last = cum[c - 1:c, :]
        k_dec = kk * jnp.exp2(last - cum)
        state_ref[hh] = state_t * jnp.exp2(last) + _bdot_tn(v16, k_dec)
        ms = jnp.mean(y * y, axis=-1, keepdims=True)
        out = y * lax.rsqrt(ms + LN_EPS) * nw_all[:, cols] * _silu(g_ref[pl.ds(r0, c), cols])
        o_ref[pl.ds(r0, c), cols] = out.astype(o_ref.dtype)

    def chunk(ci, carry):
        r0 = pl.multiple_of(ci * c, c)
        for hh in range(HG_HPS):
            head_chunk(r0, hh)
        return carry

    lax.fori_loop(0, HG_ROWS // c, chunk, 0)


def _hgrn2(proj, lb_logits, norm_w, layer):
    b, s, _ = proj.shape
    r = HG_ROWS
    w = HG_HPS * HG_DK
    nb = HG_HEADS // HG_HPS
    return pl.pallas_call(
        functools.partial(_hgrn2_kernel, layer=layer),
        grid=(b, nb, s // r),
        in_specs=[pl.BlockSpec((None, r, w), lambda bi, hi, ri: (bi, ri, hi)),
                  pl.BlockSpec((None, r, w), lambda bi, hi, ri: (bi, ri, nb + hi)),
                  pl.BlockSpec((None, r, w), lambda bi, hi, ri: (bi, ri, 2 * nb + hi)),
                  pl.BlockSpec((None, r, w), lambda bi, hi, ri: (bi, ri, 3 * nb + hi)),
                  pl.BlockSpec((DEPTH, w), lambda bi, hi, ri: (0, hi)),
                  pl.BlockSpec((1, w), lambda bi, hi, ri: (0, hi))],
        out_specs=pl.BlockSpec((None, r, w), lambda bi, hi, ri: (bi, ri, hi)),
        out_shape=jax.ShapeDtypeStruct((b, s, HG_HEADS * HG_DK), BF16),
        scratch_shapes=[pltpu.VMEM((HG_HPS, HG_DK, HG_DK), F32)],
        compiler_params=_cparams("parallel", "parallel", "arbitrary"),
    )(proj, proj, proj, proj, lb_logits, norm_w.reshape(1, HG_HEADS * HG_DK))


def _router_kernel(x_ref, w_ref, idx_ref, gate_ref, cnt_ref, carry_ref):
    tm = x_ref.shape[0]

    @pl.when(pl.program_id(0) == 0)
    def _():
        carry_ref[...] = jnp.zeros_like(carry_ref)

    logits = jnp.dot(x_ref[...], w_ref[...], preferred_element_type=F32,
                     precision=lax.Precision.HIGHEST)
    lane = lax.broadcasted_iota(jnp.int32, (tm, LANES), 1)
    lane_f = lane.astype(F32)
    lg = jnp.where(lane < N_EXPERTS, logits, NEG_BIG)
    m1 = jnp.max(lg, axis=-1, keepdims=True)
    i1 = jnp.min(jnp.where(lg == m1, lane_f, float(LANES)), axis=-1, keepdims=True).astype(jnp.int32)
    lg2 = jnp.where(lane == i1, NEG_BIG, lg)
    m2 = jnp.max(lg2, axis=-1, keepdims=True)
    i2 = jnp.min(jnp.where(lg2 == m2, lane_f, float(LANES)), axis=-1, keepdims=True).astype(jnp.int32)
    e2 = jnp.exp(m2 - m1)
    den = 1.0 + e2
    gate_ref[...] = jnp.where(lane == 0, 1.0 / den, jnp.where(lane == 1, e2 / den, 0.0))

    sel1 = lane == i1
    sel2 = lane == i2
    assign = (sel1 | sel2).astype(BF16)
    tt = lax.broadcasted_iota(jnp.int32, (tm, tm), 0)
    ss = lax.broadcasted_iota(jnp.int32, (tm, tm), 1)
    before = (ss < tt).astype(BF16)
    base = carry_ref[...] + jnp.dot(before, assign, preferred_element_type=F32)
    r1 = jnp.sum(jnp.where(sel1, base, 0.0), axis=-1, keepdims=True).astype(jnp.int32)
    r2 = jnp.sum(jnp.where(sel2, base, 0.0), axis=-1, keepdims=True).astype(jnp.int32)
    idx_ref[...] = jnp.where(lane == 0, i1, jnp.where(lane == 1, i2,
                             jnp.where(lane == 2, r1, jnp.where(lane == 3, r2, 0))))
    carry_ref[...] += jnp.sum(assign.astype(F32), axis=0, keepdims=True)
    cnt_ref[...] = carry_ref[...]


def _router(x32, w_router, tm=512):
    n, d = x32.shape
    wr = jnp.pad(w_router, ((0, 0), (0, LANES - N_EXPERTS)))
    return pl.pallas_call(
        _router_kernel,
        grid=(n // tm,),
        in_specs=[pl.BlockSpec((tm, d), lambda i: (i, 0)),
                  pl.BlockSpec((d, LANES), lambda i: (0, 0))],
        out_specs=[pl.BlockSpec((tm, LANES), lambda i: (i, 0)),
                   pl.BlockSpec((tm, LANES), lambda i: (i, 0)),
                   pl.BlockSpec((1, LANES), lambda i: (0, 0))],
        out_shape=[jax.ShapeDtypeStruct((n, LANES), jnp.int32),
                   jax.ShapeDtypeStruct((n, LANES), F32),
                   jax.ShapeDtypeStruct((1, LANES), F32)],
        scratch_shapes=[pltpu.VMEM((1, LANES), F32)],
        compiler_params=_cparams("arbitrary"),
    )(x32, wr)


def _combine_kernel(pos_ref, y_hbm, gate_ref, x_ref, g_ref, b_ref, o32_ref, o16_ref, buf_ref, sem):
    i = pl.program_id(0)
    nblk = pl.num_programs(0)
    tc = x_ref.shape[0]
    slot = i % 2

    def gather(blk, slt, wait):
        if wait:
            for k in range(2):
                pltpu.make_async_copy(y_hbm.at[pl.ds(0, tc), :], buf_ref.at[slt, k], sem.at[slt]).wait()
            return

        def body(r, carry):
            for k in range(2):
                _row_copy(y_hbm, buf_ref.at[slt, k], pos_ref[2 * (blk * tc + r) + k], r, sem.at[slt]).start()
            return carry
        lax.fori_loop(0, tc, body, 0, unroll=2)

    @pl.when(i == 0)
    def _():
        gather(0, 0, wait=False)

    @pl.when(i + 1 < nblk)
    def _():
        gather(i + 1, 1 - slot, wait=False)

    gather(i, slot, wait=True)
    gates = gate_ref[...]
    ffn = gates[:, 0:1] * buf_ref[slot, 0] + gates[:, 1:2] * buf_ref[slot, 1]
    out = _layer_norm_rows(ALPHA * x_ref[...] + ffn, g_ref[...], b_ref[...])
    o32_ref[...] = out
    o16_ref[...] = out.astype(BF16)


def _combine(pos, yr, gates, x32, ln_g, ln_b, tc=GATHER_ROWS):
    n, d = x32.shape
    return pl.pallas_call(
        _combine_kernel,
        grid_spec=pltpu.PrefetchScalarGridSpec(
            num_scalar_prefetch=1, grid=(n // tc,),
            in_specs=[pl.BlockSpec(memory_space=pl.ANY),
                      pl.BlockSpec((tc, LANES), lambda i, p: (i, 0)),
                      pl.BlockSpec((tc, d), lambda i, p: (i, 0)),
                      pl.BlockSpec((1, d), lambda i, p: (0, 0)),
                      pl.BlockSpec((1, d), lambda i, p: (0, 0))],
            out_specs=[pl.BlockSpec((tc, d), lambda i, p: (i, 0)),
                       pl.BlockSpec((tc, d), lambda i, p: (i, 0))],
            scratch_shapes=[pltpu.VMEM((2, 2, tc, d), F32), pltpu.SemaphoreType.DMA((2,))]),
        out_shape=[jax.ShapeDtypeStruct((n, d), F32), jax.ShapeDtypeStruct((n, d), BF16)],
        compiler_params=_cparams("arbitrary"),
    )(pos, yr, gates, x32, ln_g.reshape(1, d), ln_b.reshape(1, d))


def _moe(x32, w_router, w1, w3, w2, layer, ln_g, ln_b):
    n, d = x32.shape
    tm = MOE_TM
    idx, gates, counts = _router(x32, w_router)
    counts = counts[0, :N_EXPERTS].astype(jnp.int32)
    expert = idx[:, 0:2]
    rank = idx[:, 2:4]
    padded = (counts + tm - 1) // tm * tm
    padded_end = jnp.cumsum(padded)
    start_padded = padded_end - padded
    dest = start_padded[expert] + rank
    n_rows = 2 * n + N_EXPERTS * tm
    n_blocks = n_rows // tm
    token = jnp.broadcast_to(jnp.arange(n, dtype=jnp.int32)[:, None], (n, 2))
    row_token = jnp.zeros((n_rows,), jnp.int32).at[dest.reshape(-1)].set(token.reshape(-1))
    block_start = jnp.arange(n_blocks, dtype=jnp.int32) * tm
    blk_valid = (block_start < padded_end[-1]).astype(jnp.int32)
    last_expert = jnp.minimum(jnp.searchsorted(padded_end, padded_end[-1] - 1, side="right"), N_EXPERTS - 1)
    blk_expert = jnp.minimum(jnp.searchsorted(padded_end, block_start, side="right"), N_EXPERTS - 1)
    blk_expert = jnp.where(blk_valid > 0, blk_expert, last_expert).astype(jnp.int32)
    yr = _moe_ffn(x32, row_token, w1, w3, w2, layer, blk_expert, blk_valid, tm)
    return _combine(dest.reshape(-1).astype(jnp.int32), yr, gates, x32, ln_g, ln_b)


def _rope_tables(s):
    inv_freq = ROPE_BASE ** (-jnp.arange(0, RET_DK, 2, dtype=F32) / RET_DK)
    ang = jnp.arange(s, dtype=F32)[:, None] * inv_freq[None, :]
    cos = jnp.cos(ang)
    sin = jnp.sin(ang)
    return jnp.concatenate([cos, cos], axis=-1), jnp.concatenate([-sin, sin], axis=-1)


def _even_layer(x32, x16, b, s, j, w_in, ret_norm_w, conv_w, conv_b, dt_bias, a_log, d_skip,
                ssd_norm_w, w_out, ln1_g, ln1_b, w1, w3, w2, ln2_g, ln2_b):
    n = b * s
    w_dt = w_in[j, :, EV_MAIN:].reshape(D_MODEL, SSD_G, SSD_HPG)
    w_dt = jnp.pad(w_dt, ((0, 0), (0, 0), (0, LANES - SSD_HPG))).reshape(1, D_MODEL, SSD_G * LANES)
    proj = _matmul(x16, w_in, j, 1024, 1024, F32, n=EV_MAIN).reshape(b, s, EV_MAIN)
    dt_raw = _matmul(x16, w_dt, 0, 1024, SSD_G * LANES, F32).reshape(b, s, SSD_G * LANES)
    log_gamma = jnp.log1p(-jnp.exp2(-5.0 - jnp.arange(RET_HEADS, dtype=F32)))
    cos_t, sin_t = _rope_tables(s)
    ret = _retention(proj, log_gamma, cos_t, sin_t, ret_norm_w)
    ssd = _ssd(proj, dt_raw, conv_w, conv_b, _group_lanes(dt_bias), _group_lanes(a_log),
               jnp.repeat(d_skip, SSD_P).reshape(1, SSD_DI), ssd_norm_w)
    x32, x16 = _matmul_ln([ret.reshape(n, RET_V), ssd.reshape(n, SSD_DI)], w_out, j, x32, ln1_g, ln1_b)
    return _ffn_ln(x16, w1, w3, w2, j, x32, ln2_g, ln2_b)


def _odd_layer(x32, x16, b, s, layer, w_in, lb_logits, hg_norm_w, w_out, ln1_g, ln1_b,
               w_router, w1, w3, w2, ln2_g, ln2_b):
    n = b * s
    j = layer // 2
    proj = _matmul(x16, w_in, j, 1024, 1024, F32).reshape(b, s, -1)
    o = _hgrn2(proj, lb_logits, hg_norm_w, layer)
    x32, x16 = _matmul_ln([o.reshape(n, -1)], w_out, j, x32, ln1_g, ln1_b)
    return _moe(x32, w_router, w1, w3, w2, j, ln2_g, ln2_b)


def kernel(x, ev_w_in, ev_ret_norm_w, ev_conv_w, ev_conv_b, ev_dt_bias, ev_a_log, ev_d_skip,
           ev_ssd_norm_w, ev_w_out, ev_ln1_g, ev_ln1_b, ffn_w1, ffn_w3, ffn_w2, ev_ln2_g, ev_ln2_b,
           od_w_in, hg_lb_logits, od_hg_norm_w, od_w_out, od_ln1_g, od_ln1_b, moe_router,
           moe_w1, moe_w3, moe_w2, od_ln2_g, od_ln2_b):
    b, s, d = x.shape
    x32 = x.reshape(b * s, d)
    x16 = x32.astype(BF16)
    ev_w_in, ev_w_out, ffn_w1, ffn_w3, ffn_w2, od_w_in, od_w_out, moe_w1, moe_w3, moe_w2 = (
        w.astype(BF16) for w in (ev_w_in, ev_w_out, ffn_w1, ffn_w3, ffn_w2, od_w_in, od_w_out,
                                 moe_w1, moe_w3, moe_w2))
    for layer in range(DEPTH):
        j = layer // 2
        if layer % 2 == 0:
            x32, x16 = _even_layer(x32, x16, b, s, j, ev_w_in, ev_ret_norm_w[j], ev_conv_w[j], ev_conv_b[j],
                                   ev_dt_bias[j], ev_a_log[j], ev_d_skip[j], ev_ssd_norm_w[j], ev_w_out,
                                   ev_ln1_g[j], ev_ln1_b[j], ffn_w1, ffn_w3, ffn_w2,
                                   ev_ln2_g[j], ev_ln2_b[j])
        else:
            x32, x16 = _odd_layer(x32, x16, b, s, layer, od_w_in, hg_lb_logits, od_hg_norm_w[j],
                                  od_w_out, od_ln1_g[j], od_ln1_b[j], moe_router[j],
                                  moe_w1, moe_w3, moe_w2, od_ln2_g[j], od_ln2_b[j])
    return x32.reshape(b, s, d)
```
